```python
import jax, jax.numpy as jnp
from jax import lax
import numpy as np

D_MODEL = 2048
BATCH = 8
SEQ = 2048
DEPTH = 1
DEC_BATCH = 32
DEC_SEQ = 4
PAST_LEN = 16384
PAGE_SIZE = 128

MIX_WIDTH = D_MODEL
RET_WIDTH = MIX_WIDTH // 2
FOX_WIDTH = MIX_WIDTH - RET_WIDTH
RET_HEADS = 4
RET_HEAD_DIM = RET_WIDTH // RET_HEADS
FOX_HEADS = 8
FOX_HEAD_DIM = FOX_WIDTH // FOX_HEADS
D_FF = 5632
RET_CHUNK = 128
FOX_BLOCK = 128
ROPE_BASE = 10000.0
FGATE_BIAS_LO = 2.0
FGATE_BIAS_HI = 10.0
LN_EPS = 1e-5
GN_EPS = 1e-6
ALPHA = (2 * DEPTH) ** 0.25
BETA = (8 * DEPTH) ** -0.25
MIX_COLS = 4 * RET_WIDTH + 3 * FOX_WIDTH + FOX_HEADS

kernel_name = 'hybrid_retention_fox_macaron_deepnorm_step'


def layer_norm(x, g, b):
    xf = x.astype(jnp.float32)
    mu = jnp.mean(xf, axis=-1, keepdims=True)
    xc = xf - mu
    var = jnp.mean(jnp.square(xc), axis=-1, keepdims=True)
    y = xc * lax.rsqrt(var + LN_EPS) * g.astype(jnp.float32) + b.astype(jnp.float32)
    return y.astype(x.dtype)


def swiglu(x, wg, wu, wd):
    return (jax.nn.silu(x @ wg) * (x @ wu)) @ wd


def ffn_sublayer(x, wg, wu, wd, g, b):
    return layer_norm(ALPHA * x + 0.5 * swiglu(x, wg, wu, wd), g, b)


def rope(x, pos):
    half = x.shape[-1] // 2
    inv = ROPE_BASE ** (-jnp.arange(half, dtype=jnp.float32) / half)
    ang = pos.astype(jnp.float32)[:, None] * inv[None, :]
    cos = jnp.cos(ang)[None, :, None, :]
    sin = jnp.sin(ang)[None, :, None, :]
    x1 = x[..., :half].astype(jnp.float32)
    x2 = x[..., half:].astype(jnp.float32)
    return jnp.concatenate([x1 * cos - x2 * sin, x1 * sin + x2 * cos], axis=-1).astype(x.dtype)


def project(h, w_in, b_f, pos):
    B, L, _ = h.shape
    z = h @ w_in
    splits = [RET_WIDTH, 2 * RET_WIDTH, 3 * RET_WIDTH, 4 * RET_WIDTH,
              4 * RET_WIDTH + FOX_WIDTH, 4 * RET_WIDTH + 2 * FOX_WIDTH, 4 * RET_WIDTH + 3 * FOX_WIDTH]
    rq, rk, rv, rg, fq, fk, fv, fl = jnp.split(z, splits, axis=-1)
    rq = rope(rq.reshape(B, L, RET_HEADS, RET_HEAD_DIM), pos)
    rk = rope(rk.reshape(B, L, RET_HEADS, RET_HEAD_DIM), pos) * (RET_HEAD_DIM ** -0.5)
    rv = rv.reshape(B, L, RET_HEADS, RET_HEAD_DIM)
    fq = fq.reshape(B, L, FOX_HEADS, FOX_HEAD_DIM)
    fk = fk.reshape(B, L, FOX_HEADS, FOX_HEAD_DIM)
    fv = fv.reshape(B, L, FOX_HEADS, FOX_HEAD_DIM)
    logf = jax.nn.log_sigmoid(fl.astype(jnp.float32) + b_f.astype(jnp.float32))
    return rq, rk, rv, rg, fq, fk, fv, logf


def retention(q, k, v, s0):
    B, L, H, dk = q.shape
    dv = v.shape[-1]
    C = RET_CHUNK if L % RET_CHUNK == 0 else L
    n = L // C
    log_g = jnp.log1p(-jnp.exp2(-5.0 - jnp.arange(H, dtype=jnp.float32)))
    idx = jnp.arange(C, dtype=jnp.float32)
    rel = idx[:, None] - idx[None, :]
    inner_decay = jnp.where(rel >= 0, jnp.exp(log_g[:, None, None] * jnp.maximum(rel, 0.0)), 0.0)
    q_decay = jnp.exp(log_g[None, :] * (idx[:, None] + 1.0))[None, :, :, None]
    k_decay = jnp.exp(log_g[None, :] * (C - 1.0 - idx[:, None]))[None, :, :, None]
    chunk_decay = jnp.exp(log_g * C)[None, :, None, None]

    def to_chunks(t):
        return jnp.moveaxis(t.astype(jnp.float32).reshape(B, n, C, H, t.shape[-1]), 1, 0)

    def step(S, blk):
        qb, kb, vb = blk
        att = jnp.einsum('bchd,bshd->bhcs', qb, kb) * inner_decay[None]
        o = jnp.einsum('bhcs,bshv->bchv', att, vb) + jnp.einsum('bchd,bhdv->bchv', qb, S) * q_decay
        S = chunk_decay * S + jnp.einsum('bchd,bchv->bhdv', kb * k_decay, vb)
        return S, o

    S, o = lax.scan(step, s0, (to_chunks(q), to_chunks(k), to_chunks(v)))
    return jnp.moveaxis(o, 0, 1).reshape(B, L, H, dv), S


def fox_prompt(q, k, v, logf):
    B, S, H, d = q.shape
    scale = FOX_HEAD_DIM ** -0.5
    c = jnp.cumsum(logf, axis=1).transpose(0, 2, 1)
    kpos = jnp.arange(S)
    nb = S // FOX_BLOCK

    def block(i):
        start = i * FOX_BLOCK
        qb = lax.dynamic_slice_in_dim(q, start, FOX_BLOCK, axis=1)
        cb = lax.dynamic_slice_in_dim(c, start, FOX_BLOCK, axis=2)
        s = jnp.einsum('bqhd,bkhd->bhqk', qb, k, preferred_element_type=jnp.float32) * scale
        s = s + cb[..., :, None] - c[..., None, :]
        qpos = start + jnp.arange(FOX_BLOCK)
        mask = kpos[None, :] <= qpos[:, None]
        p = jax.nn.softmax(jnp.where(mask[None, None], s, -jnp.inf), axis=-1)
        return jnp.einsum('bhqk,bkhd->bqhd', p.astype(v.dtype), v)

    out = lax.map(block, jnp.arange(nb))
    return jnp.moveaxis(out, 0, 1).reshape(B, S, H, d)


def fox_sample(q, k, v, logf, cache_k, cache_v, cache_logf, page_table):
    DB, L, H, d = q.shape
    scale = FOX_HEAD_DIM ** -0.5
    kp = cache_k[page_table].reshape(DB, -1, H, d)
    vp = cache_v[page_table].reshape(DB, -1, H, d)
    lp = cache_logf[page_table].reshape(DB, -1, H).astype(jnp.float32)
    P = kp.shape[1]
    suffix = (jnp.cumsum(lp[:, ::-1], axis=1)[:, ::-1] - lp).transpose(0, 2, 1)
    cn = jnp.cumsum(logf, axis=1).transpose(0, 2, 1)
    s_past = jnp.einsum('bqhd,bkhd->bhqk', q, kp, preferred_element_type=jnp.float32) * scale
    s_past = s_past + cn[..., :, None] + suffix[..., None, :]
    s_new = jnp.einsum('bqhd,bkhd->bhqk', q, k, preferred_element_type=jnp.float32) * scale
    s_new = s_new + cn[..., :, None] - cn[..., None, :]
    causal = jnp.arange(L)[None, :] <= jnp.arange(L)[:, None]
    s_new = jnp.where(causal[None, None], s_new, -jnp.inf)
    p = jax.nn.softmax(jnp.concatenate([s_past, s_new], axis=-1), axis=-1)
    return (jnp.einsum('bhqk,bkhd->bqhd', p[..., :P].astype(vp.dtype), vp)
            + jnp.einsum('bhqk,bkhd->bqhd', p[..., P:].astype(v.dtype), v))


def mix_output(ret_o, rg, fox_o, w_out):
    B, L = rg.shape[:2]
    rn = ret_o * lax.rsqrt(jnp.mean(jnp.square(ret_o), axis=-1, keepdims=True) + GN_EPS)
    ret = rn.reshape(B, L, RET_WIDTH).astype(rg.dtype) * jax.nn.silu(rg)
    fox = fox_o.reshape(B, L, FOX_WIDTH).astype(rg.dtype)
    return jnp.concatenate([ret, fox], axis=-1) @ w_out


def setup_inputs(seed: int = 0) -> dict:
    key = jax.random.key(seed)
    ks = jax.random.split(key, 24)
    n_pages = PAST_LEN // PAGE_SIZE
    n_pool = (DEC_BATCH * n_pages * 5) // 4
    nrm = jax.random.normal
    head_bias = jnp.linspace(FGATE_BIAS_LO, FGATE_BIAS_HI, FOX_HEADS, dtype=jnp.float32)
    x_prompt = nrm(ks[0], (BATCH, SEQ, D_MODEL), jnp.float32)
    x_sample = nrm(ks[1], (DEC_BATCH, DEC_SEQ, D_MODEL), jnp.float32)
    cache_k = nrm(ks[2], (DEPTH, n_pool, PAGE_SIZE, FOX_HEADS, FOX_HEAD_DIM), jnp.float32)
    cache_v = BETA * nrm(ks[3], (DEPTH, n_pool, PAGE_SIZE, FOX_HEADS, FOX_HEAD_DIM), jnp.float32)
    cache_logf = jax.nn.log_sigmoid(head_bias + nrm(ks[4], (DEPTH, n_pool, PAGE_SIZE, FOX_HEADS), jnp.float32))
    state_ret = nrm(ks[5], (DEPTH, DEC_BATCH, RET_HEADS, RET_HEAD_DIM, RET_HEAD_DIM), jnp.float32)
    page_table = jax.random.permutation(ks[6], n_pool)[:DEC_BATCH * n_pages].reshape(DEC_BATCH, n_pages).astype(jnp.int32)

    def ln_pair(k):
        k1, k2 = jax.random.split(k)
        return (1.0 + 0.02 * nrm(k1, (DEPTH, D_MODEL), jnp.float32),
                0.02 * nrm(k2, (DEPTH, D_MODEL), jnp.float32))

    def ffn_w(k):
        k1, k2, k3 = jax.random.split(k, 3)
        return (BETA * D_MODEL ** -0.5 * nrm(k1, (DEPTH, D_MODEL, D_FF), jnp.float32),
                BETA * D_MODEL ** -0.5 * nrm(k2, (DEPTH, D_MODEL, D_FF), jnp.float32),
                BETA * D_FF ** -0.5 * nrm(k3, (DEPTH, D_FF, D_MODEL), jnp.float32))

    ln1_g, ln1_b = ln_pair(ks[7])
    w_ffn1_gate, w_ffn1_up, w_ffn1_down = ffn_w(ks[8])
    ln2_g, ln2_b = ln_pair(ks[9])
    col_scale = jnp.concatenate([
        jnp.ones((2 * RET_WIDTH,), jnp.float32), BETA * jnp.ones((RET_WIDTH,), jnp.float32),
        jnp.ones((RET_WIDTH + 2 * FOX_WIDTH,), jnp.float32), BETA * jnp.ones((FOX_WIDTH,), jnp.float32),
        jnp.ones((FOX_HEADS,), jnp.float32)])
    w_in = D_MODEL ** -0.5 * nrm(ks[10], (DEPTH, D_MODEL, MIX_COLS), jnp.float32) * col_scale
    b_f = head_bias + 0.1 * nrm(ks[11], (DEPTH, FOX_HEADS), jnp.float32)
    w_out = BETA * MIX_WIDTH ** -0.5 * nrm(ks[12], (DEPTH, MIX_WIDTH, D_MODEL), jnp.float32)
    ln3_g, ln3_b = ln_pair(ks[13])
    w_ffn2_gate, w_ffn2_up, w_ffn2_down = ffn_w(ks[14])
    return {'x_prompt': x_prompt, 'x_sample': x_sample,
            'cache_k': cache_k, 'cache_v': cache_v, 'cache_logf': cache_logf, 'state_ret': state_ret,
            'page_table': page_table,
            'ln1_g': ln1_g, 'ln1_b': ln1_b,
            'w_ffn1_gate': w_ffn1_gate, 'w_ffn1_up': w_ffn1_up, 'w_ffn1_down': w_ffn1_down,
            'ln2_g': ln2_g, 'ln2_b': ln2_b,
            'w_in': w_in, 'b_f': b_f, 'w_out': w_out,
            'ln3_g': ln3_g, 'ln3_b': ln3_b,
            'w_ffn2_gate': w_ffn2_gate, 'w_ffn2_up': w_ffn2_up, 'w_ffn2_down': w_ffn2_down}


def reference(x_prompt, x_sample, cache_k, cache_v, cache_logf, state_ret, page_table,
              ln1_g, ln1_b, w_ffn1_gate, w_ffn1_up, w_ffn1_down, ln2_g, ln2_b,
              w_in, b_f, w_out, ln3_g, ln3_b, w_ffn2_gate, w_ffn2_up, w_ffn2_down):
    xp, xs = x_prompt, x_sample
    B, S = xp.shape[:2]
    DB, L = xs.shape[:2]
    past_len = page_table.shape[1] * cache_k.shape[2]
    pos_p = jnp.arange(S)
    pos_s = past_len + jnp.arange(L)
    rs_p, k_p, v_p, lf_p, rs_s, k_s, v_s, lf_s = [], [], [], [], [], [], [], []
    for l in range(DEPTH):
        xp = ffn_sublayer(xp, w_ffn1_gate[l], w_ffn1_up[l], w_ffn1_down[l], ln1_g[l], ln1_b[l])
        xs = ffn_sublayer(xs, w_ffn1_gate[l], w_ffn1_up[l], w_ffn1_down[l], ln1_g[l], ln1_b[l])
        rq, rk, rv, rg, fq, fk, fv, lf = project(xp, w_in[l], b_f[l], pos_p)
        ret_o, s_fin = retention(rq, rk, rv, jnp.zeros((B, RET_HEADS, RET_HEAD_DIM, RET_HEAD_DIM), jnp.float32))
        fox_o = fox_prompt(fq, fk, fv, lf)
        xp = layer_norm(ALPHA * xp + mix_output(ret_o, rg, fox_o, w_out[l]), ln2_g[l], ln2_b[l])
        rs_p.append(s_fin); k_p.append(fk); v_p.append(fv); lf_p.append(lf)
        rq, rk, rv, rg, fq, fk, fv, lf = project(xs, w_in[l], b_f[l], pos_s)
        ret_o, s_new = retention(rq, rk, rv, state_ret[l].astype(jnp.float32))
        fox_o = fox_sample(fq, fk, fv, lf, cache_k[l], cache_v[l], cache_logf[l], page_table)
        xs = layer_norm(ALPHA * xs + mix_output(ret_o, rg, fox_o, w_out[l]), ln2_g[l], ln2_b[l])
        rs_s.append(s_new); k_s.append(fk); v_s.append(fv); lf_s.append(lf)
        xp = ffn_sublayer(xp, w_ffn2_gate[l], w_ffn2_up[l], w_ffn2_down[l], ln3_g[l], ln3_b[l])
        xs = ffn_sublayer(xs, w_ffn2_gate[l], w_ffn2_up[l], w_ffn2_down[l], ln3_g[l], ln3_b[l])
    return (xp, xs,
            jnp.stack(rs_p), jnp.stack(k_p), jnp.stack(v_p), jnp.stack(lf_p),
            jnp.stack(rs_s), jnp.stack(k_s), jnp.stack(v_s), jnp.stack(lf_s))
```

```python
import functools

import numpy as np
import jax
import jax.numpy as jnp
from jax import lax
from jax.experimental import pallas as pl
from jax.experimental.pallas import tpu as pltpu

F32 = jnp.float32
BF16 = jnp.bfloat16

RET_HEADS = 4
FOX_HEADS = 8
RET_CHUNK = 128
ROPE_BASE = 10000.0
LN_EPS = 1e-5
GN_EPS = 1e-6

LANES = 128
V7X_VMEM_BYTES = 64 * 1024 * 1024
COMPILER_SCRATCH_BYTES = 12 * 1024 * 1024


def _nbytes(shape, dtype):
    return int(np.prod(shape)) * jnp.dtype(dtype).itemsize


def _params(semantics, block_bytes, scratch_bytes=0):
    need = 2 * block_bytes + scratch_bytes + COMPILER_SCRATCH_BYTES
    limit = int(min(need, V7X_VMEM_BYTES - 4 * 1024 * 1024))
    return pltpu.CompilerParams(dimension_semantics=semantics, vmem_limit_bytes=limit)


def _layer_norm(y, g, b):
    mu = jnp.mean(y, axis=-1, keepdims=True)
    yc = y - mu
    var = jnp.mean(yc * yc, axis=-1, keepdims=True)
    return yc * lax.rsqrt(var + LN_EPS) * g + b


def _silu(x):
    return x * jax.nn.sigmoid(x)


def _log_sigmoid(x):
    return jnp.minimum(x, 0.0) - jnp.log1p(jnp.exp(-jnp.abs(x)))


def _dot(a, b):
    return jnp.dot(a, b, preferred_element_type=F32)


def _dot_nt(a, b):
    return lax.dot_general(a, b, (((1,), (1,)), ((), ())), preferred_element_type=F32)


def _dot_tn(a, b):
    return lax.dot_general(a, b, (((0,), (0,)), ((), ())), preferred_element_type=F32)


def _cumsum(x, axis, period=None):
    n = x.shape[axis]
    idx = lax.broadcasted_iota(jnp.int32, x.shape, axis)
    if period is not None:
        idx = idx % period
        n = period
    shift = 1
    while shift < n:
        x = x + jnp.where(idx >= shift, pltpu.roll(x, shift, axis), 0.0)
        shift *= 2
    return x


def _ffn_kernel(x_ref, wg_ref, wu_ref, wd_ref, g_ref, b_ref, *rest, alpha, emit_bf16):
    if emit_bf16:
        o_ref, ob_ref, xb_ref, acc_ref = rest
    else:
        o_ref, xb_ref, acc_ref = rest
    f = pl.program_id(1)

    @pl.when(f == 0)
    def _():
        xb_ref[...] = x_ref[...].astype(BF16)
        acc_ref[...] = jnp.zeros_like(acc_ref)

    xb = xb_ref[...]
    hidden = _silu(_dot(xb, wg_ref[...])) * _dot(xb, wu_ref[...])
    acc_ref[...] += _dot(hidden.astype(BF16), wd_ref[...])

    @pl.when(f == pl.num_programs(1) - 1)
    def _():
        y = _layer_norm(alpha * x_ref[...] + 0.5 * acc_ref[...], g_ref[...], b_ref[...])
        o_ref[...] = y
        if emit_bf16:
            ob_ref[...] = y.astype(BF16)


def _ffn_sublayer(x, wg, wu, wd, g, b, *, alpha, tm, tf, emit_bf16):
    T, D = x.shape
    F = wg.shape[1]
    assert T % tm == 0 and F % tf == 0
    out_shape = [jax.ShapeDtypeStruct((T, D), F32)]
    out_specs = [pl.BlockSpec((tm, D), lambda i, f: (i, 0))]
    if emit_bf16:
        out_shape.append(jax.ShapeDtypeStruct((T, D), BF16))
        out_specs.append(pl.BlockSpec((tm, D), lambda i, f: (i, 0)))
    block_bytes = (2 * _nbytes((tm, D), F32) + int(emit_bf16) * _nbytes((tm, D), BF16)
                   + 3 * _nbytes((D, tf), BF16))
    scratch_bytes = _nbytes((tm, D), BF16) + _nbytes((tm, D), F32)
    return pl.pallas_call(
        functools.partial(_ffn_kernel, alpha=alpha, emit_bf16=emit_bf16),
        grid=(T // tm, F // tf),
        in_specs=[pl.BlockSpec((tm, D), lambda i, f: (i, 0)),
                  pl.BlockSpec((D, tf), lambda i, f: (0, f)),
                  pl.BlockSpec((D, tf), lambda i, f: (0, f)),
                  pl.BlockSpec((tf, D), lambda i, f: (f, 0)),
                  pl.BlockSpec((1, D), lambda i, f: (0, 0)),
                  pl.BlockSpec((1, D), lambda i, f: (0, 0))],
        out_specs=out_specs,
        out_shape=out_shape,
        scratch_shapes=[pltpu.VMEM((tm, D), BF16), pltpu.VMEM((tm, D), F32)],
        compiler_params=_params(("parallel", "arbitrary"), block_bytes, scratch_bytes),
        name="ffn_sublayer",
    )(x, wg, wu, wd, g, b)


def _proj_rope_kernel(h_ref, w_ref, cos_ref, sin_ref, o_ref, *, heads, k_scale):
    scale = jnp.where(pl.program_id(0) == 0, 1.0, k_scale).astype(F32)
    z = _dot(h_ref[...], w_ref[...])
    cos = cos_ref[...]
    sin = sin_ref[...]
    hd = z.shape[1] // heads
    half = hd // 2
    for h in range(heads):
        x1 = z[:, h * hd:h * hd + half]
        x2 = z[:, h * hd + half:(h + 1) * hd]
        o_ref[:, h * hd:h * hd + half] = ((x1 * cos - x2 * sin) * scale).astype(o_ref.dtype)
        o_ref[:, h * hd + half:(h + 1) * hd] = ((x1 * sin + x2 * cos) * scale).astype(o_ref.dtype)


def _proj_rope(hb, w, cos, sin, *, tm, heads, k_scale):
    T, D = hb.shape
    N = w.shape[1]
    tn = N // 2
    half = cos.shape[1]
    pos_blocks = cos.shape[0] // tm
    block_bytes = (_nbytes((tm, D), BF16) + _nbytes((D, tn), BF16) + 2 * _nbytes((tm, half), F32)
                   + _nbytes((tm, tn), BF16))
    return pl.pallas_call(
        functools.partial(_proj_rope_kernel, heads=heads, k_scale=k_scale),
        grid=(2, T // tm),
        in_specs=[pl.BlockSpec((tm, D), lambda j, i: (i, 0)),
                  pl.BlockSpec((D, tn), lambda j, i: (0, j)),
                  pl.BlockSpec((tm, half), lambda j, i: (i % pos_blocks, 0)),
                  pl.BlockSpec((tm, half), lambda j, i: (i % pos_blocks, 0))],
        out_specs=pl.BlockSpec((tm, tn), lambda j, i: (i, j)),
        out_shape=jax.ShapeDtypeStruct((T, N), BF16),
        compiler_params=_params(("parallel", "parallel"), block_bytes),
        name="proj_rope",
    )(hb, w, cos, sin)


def _proj_kernel(h_ref, w_ref, o_ref):
    o_ref[...] = _dot(h_ref[...], w_ref[...]).astype(o_ref.dtype)


def _proj(hb, w, *, tm, tn):
    T, D = hb.shape
    N = w.shape[1]
    block_bytes = _nbytes((tm, D), BF16) + _nbytes((D, tn), BF16) + _nbytes((tm, tn), BF16)
    return pl.pallas_call(
        _proj_kernel,
        grid=(N // tn, T // tm),
        in_specs=[pl.BlockSpec((tm, D), lambda j, i: (i, 0)),
                  pl.BlockSpec((D, tn), lambda j, i: (0, j))],
        out_specs=pl.BlockSpec((tm, tn), lambda j, i: (i, j)),
        out_shape=jax.ShapeDtypeStruct((T, N), BF16),
        compiler_params=_params(("parallel", "parallel"), block_bytes),
        name="proj",
    )(hb, w)


def _proj_rows_kernel(h_ref, w_ref, rows_ref, ob_ref, *, heads):
    z = _dot(h_ref[...], w_ref[...])
    ob_ref[...] = z.astype(ob_ref.dtype)
    hd = z.shape[1] // heads
    for h in range(heads):
        rows_ref[:, h, :] = z[:, h * hd:(h + 1) * hd]


def _proj_rows(hb, w, *, tm, heads):
    T, D = hb.shape
    N = w.shape[1]
    hd = N // heads
    block_bytes = (_nbytes((tm, D), BF16) + _nbytes((D, N), BF16) + _nbytes((tm, N), F32)
                   + _nbytes((tm, N), BF16))
    return pl.pallas_call(
        functools.partial(_proj_rows_kernel, heads=heads),
        grid=(T // tm,),
        in_specs=[pl.BlockSpec((tm, D), lambda i: (i, 0)),
                  pl.BlockSpec((D, N), lambda i: (0, 0))],
        out_specs=[pl.BlockSpec((tm, heads, hd), lambda i: (i, 0, 0)),
                   pl.BlockSpec((tm, N), lambda i: (i, 0))],
        out_shape=[jax.ShapeDtypeStruct((T, heads, hd), F32),
                   jax.ShapeDtypeStruct((T, N), BF16)],
        compiler_params=_params(("parallel",), block_bytes),
        name="proj_rows",
    )(hb, w)


def _fgate_prompt_kernel(h_ref, wf_ref, wft_ref, brow_ref, bcol_ref, logf_ref, c_ref, ct_ref):
    h = h_ref[0]
    logf = _log_sigmoid(_dot(h, wf_ref[...]) + brow_ref[...])
    logf_ref[0] = logf
    c_ref[0] = _cumsum(logf, 0)
    logf_t = _log_sigmoid(_dot_nt(wft_ref[...], h) + bcol_ref[...])
    ct_ref[0] = _cumsum(logf_t, 1)


def _fgate_prompt(hb3, wf, wft, brow, bcol):
    B, S, D = hb3.shape
    H = wf.shape[1]
    block_bytes = _nbytes((S, D), BF16) + 3 * _nbytes((S, LANES), F32)
    return pl.pallas_call(
        _fgate_prompt_kernel,
        grid=(B,),
        in_specs=[pl.BlockSpec((1, S, D), lambda b: (b, 0, 0)),
                  pl.BlockSpec((D, H), lambda b: (0, 0)),
                  pl.BlockSpec((H, D), lambda b: (0, 0)),
                  pl.BlockSpec((1, H), lambda b: (0, 0)),
                  pl.BlockSpec((H, 1), lambda b: (0, 0))],
        out_specs=[pl.BlockSpec((1, S, H), lambda b: (b, 0, 0)),
                   pl.BlockSpec((1, S, H), lambda b: (b, 0, 0)),
                   pl.BlockSpec((1, H, S), lambda b: (b, 0, 0))],
        out_shape=[jax.ShapeDtypeStruct((B, S, H), F32),
                   jax.ShapeDtypeStruct((B, S, H), F32),
                   jax.ShapeDtypeStruct((B, H, S), F32)],
        compiler_params=_params(("parallel",), block_bytes),
        name="fgate_prompt",
    )(hb3, wf, wft, brow, bcol)


def _fgate_sample_kernel(h_ref, wf_ref, brow_ref, logf_ref, cn_ref, *, seq):
    logf = _log_sigmoid(_dot(h_ref[...], wf_ref[...]) + brow_ref[...])
    logf_ref[...] = logf
    cn_ref[...] = _cumsum(logf, 0, period=seq)


def _fgate_sample(hb, wf, brow, *, seq):
    T, _ = hb.shape
    H = wf.shape[1]
    return pl.pallas_call(
        functools.partial(_fgate_sample_kernel, seq=seq),
        out_shape=[jax.ShapeDtypeStruct((T, H), F32), jax.ShapeDtypeStruct((T, H), F32)],
        name="fgate_sample",
    )(hb, wf, brow)


def _ret_log_gamma(h):
    vals = [float(np.log1p(-np.exp2(-5.0 - i))) for i in range(RET_HEADS)]
    out = jnp.float32(vals[-1])
    for i in reversed(range(RET_HEADS - 1)):
        out = jnp.where(h == i, jnp.float32(vals[i]), out)
    return out


def _ret_decays(log_g, C):
    row = lax.broadcasted_iota(jnp.int32, (C, C), 0)
    col = lax.broadcasted_iota(jnp.int32, (C, C), 1)
    rel = (row - col).astype(F32)
    inner = jnp.where(rel >= 0, jnp.exp(log_g * jnp.maximum(rel, 0.0)), 0.0)
    idx = lax.broadcasted_iota(jnp.int32, (C, 1), 0).astype(F32)
    q_decay = jnp.exp(log_g * (idx + 1.0))
    k_decay = jnp.exp(log_g * (C - 1.0 - idx))
    chunk_decay = jnp.exp(log_g * jnp.full((1, 1), float(C), F32))
    return inner, q_decay, k_decay, chunk_decay


def _ret_chunk(q, k, v, g, state, decays):
    inner, q_decay, k_decay, chunk_decay = decays
    att = _dot_nt(q, k) * inner
    o = _dot(att.astype(BF16), v) + _dot(q, state.astype(BF16)) * q_decay
    kd = (k.astype(F32) * k_decay).astype(BF16)
    new_state = chunk_decay * state + _dot_tn(kd, v)
    rn = o * lax.rsqrt(jnp.mean(o * o, axis=-1, keepdims=True) + GN_EPS)
    return (rn * _silu(g.astype(F32))).astype(BF16), new_state


def _ret_prompt_kernel(q_ref, k_ref, v_ref, g_ref, o_ref, s_ref, *, chunk):
    S = q_ref.shape[0]
    decays = _ret_decays(_ret_log_gamma(pl.program_id(1)), chunk)
    s_ref[0, 0] = jnp.zeros(s_ref.shape[2:], F32)

    def body(c, carry):
        sl = pl.ds(pl.multiple_of(c * chunk, chunk), chunk)
        out, new_state = _ret_chunk(q_ref[sl, :], k_ref[sl, :], v_ref[sl, :], g_ref[sl, :], s_ref[0, 0], decays)
        o_ref[sl, :] = out
        s_ref[0, 0] = new_state
        return carry

    lax.fori_loop(0, S // chunk, body, 0)


def _ret_prompt(zr, zp, *, B, S):
    W = zr.shape[1] // 2
    hd = W // RET_HEADS
    block_bytes = 5 * _nbytes((S, hd), BF16) + _nbytes((hd, hd), F32)
    return pl.pallas_call(
        functools.partial(_ret_prompt_kernel, chunk=RET_CHUNK if S % RET_CHUNK == 0 else S),
        grid=(B, RET_HEADS),
        in_specs=[pl.BlockSpec((S, hd), lambda b, h: (b, h)),
                  pl.BlockSpec((S, hd), lambda b, h: (b, RET_HEADS + h)),
                  pl.BlockSpec((S, hd), lambda b, h: (b, h)),
                  pl.BlockSpec((S, hd), lambda b, h: (b, RET_HEADS + h))],
        out_specs=[pl.BlockSpec((S, hd), lambda b, h: (b, h)),
                   pl.BlockSpec((1, 1, hd, hd), lambda b, h: (b, h, 0, 0))],
        out_shape=[jax.ShapeDtypeStruct((B * S, W), BF16),
                   jax.ShapeDtypeStruct((B, RET_HEADS, hd, hd), F32)],
        compiler_params=_params(("parallel", "parallel"), block_bytes),
        name="retention_prompt",
    )(zr, zr, zp, zp)


def _ret_sample_kernel(zr_ref, zp_ref, s0_ref, o_ref, s_ref, *, width):
    L = zr_ref.shape[1]
    hd = width // RET_HEADS
    for h in range(RET_HEADS):
        decays = _ret_decays(jnp.float32(np.log1p(-np.exp2(-5.0 - h))), L)
        q = zr_ref[0, :, h * hd:(h + 1) * hd]
        k = zr_ref[0, :, width + h * hd:width + (h + 1) * hd]
        v = zp_ref[0, :, h * hd:(h + 1) * hd]
        g = zp_ref[0, :, width + h * hd:width + (h + 1) * hd]
        out, new_state = _ret_chunk(q, k, v, g, s0_ref[0, 0, h], decays)
        o_ref[0, :, h * hd:(h + 1) * hd] = out
        s_ref[0, h] = new_state


def _ret_sample(zr3, zp3, state, layer):
    DB, L, W2 = zr3.shape
    W = W2 // 2
    hd = W // RET_HEADS
    NP = zp3.shape[2]
    block_bytes = _nbytes((16, W2 + NP + W), BF16) + 2 * _nbytes((RET_HEADS, hd, hd), F32)
    return pl.pallas_call(
        functools.partial(_ret_sample_kernel, width=W),
        grid=(DB,),
        in_specs=[pl.BlockSpec((1, L, W2), lambda b: (b, 0, 0)),
                  pl.BlockSpec((1, L, NP), lambda b: (b, 0, 0)),
                  pl.BlockSpec((1, 1, RET_HEADS, hd, hd), lambda b: (layer, b, 0, 0, 0))],
        out_specs=[pl.BlockSpec((1, L, W), lambda b: (b, 0, 0)),
                   pl.BlockSpec((1, RET_HEADS, hd, hd), lambda b: (b, 0, 0, 0))],
        out_shape=[jax.ShapeDtypeStruct((DB, L, W), BF16),
                   jax.ShapeDtypeStruct((DB, RET_HEADS, hd, hd), F32)],
        compiler_params=_params(("parallel",), block_bytes),
        name="retention_sample",
    )(zr3, zp3, state)


def _fox_prompt_kernel(q_ref, k_ref, v_ref, c_ref, ct_ref, o_ref, *, heads, scale):
    tq = q_ref.shape[1]
    hd = q_ref.shape[2] // heads
    qi = pl.program_id(1)
    row = lax.broadcasted_iota(jnp.int32, (tq, tq), 0)
    col = lax.broadcasted_iota(jnp.int32, (tq, tq), 1)
    causal = col <= row

    for h in range(heads):
        cols = slice(h * hd, (h + 1) * hd)
        q = q_ref[0, :, cols]
        c_q = c_ref[0, :, h:h + 1]

        def scores(j, masked):
            ks = pl.ds(pl.multiple_of(j * tq, tq), tq)
            s = _dot_nt(q, k_ref[0, ks, cols]) * scale + (c_q - ct_ref[0, h:h + 1, ks])
            if masked:
                s = jnp.where(causal, s, -jnp.inf)
            return s, v_ref[0, ks, cols]

        def update(carry, s, v):
            m, l, acc = carry
            m_new = jnp.maximum(m, jnp.max(s, axis=-1, keepdims=True))
            p = jnp.exp(s - m_new)
            a = jnp.exp(m - m_new)
            return m_new, a * l + jnp.sum(p, axis=-1, keepdims=True), a * acc + _dot(p.astype(BF16), v)

        s, v = scores(qi, True)
        m0 = jnp.max(s, axis=-1, keepdims=True)
        p0 = jnp.exp(s - m0)
        carry = (m0, jnp.sum(p0, axis=-1, keepdims=True), _dot(p0.astype(BF16), v))
        carry = lax.fori_loop(0, qi, lambda j, cr: update(cr, *scores(j, False)), carry)
        _, l, acc = carry
        o_ref[0, :, cols] = (acc / l).astype(o_ref.dtype)


def _fox_prompt(fq, fkb, fvb, c, ct, *, tq, q_col):
    B, S, H = c.shape
    W = fkb.shape[2]
    scale = float((W // H) ** -0.5)
    block_bytes = (2 * _nbytes((tq, W), BF16) + 2 * _nbytes((S, W), BF16)
                   + _nbytes((tq, LANES), F32) + _nbytes((H, S), F32))
    return pl.pallas_call(
        functools.partial(_fox_prompt_kernel, heads=H, scale=scale),
        grid=(B, S // tq),
        in_specs=[pl.BlockSpec((1, tq, W), lambda b, i: (b, i, q_col)),
                  pl.BlockSpec((1, S, W), lambda b, i: (b, 0, 0)),
                  pl.BlockSpec((1, S, W), lambda b, i: (b, 0, 0)),
                  pl.BlockSpec((1, tq, H), lambda b, i: (b, i, 0)),
                  pl.BlockSpec((1, H, S), lambda b, i: (b, 0, 0))],
        out_specs=pl.BlockSpec((1, tq, W), lambda b, i: (b, i, 0)),
        out_shape=jax.ShapeDtypeStruct((B, S, W), BF16),
        compiler_params=_params(("parallel", "parallel"), block_bytes),
        name="fox_prompt",
    )(fq, fkb, fvb, c, ct)


def _suffix_kernel(x_ref, m_ref, o_ref):
    x = x_ref[...]
    hi = x.astype(BF16)
    r1 = x - hi.astype(F32)
    mid = r1.astype(BF16)
    lo = (r1 - mid.astype(F32)).astype(BF16)
    m = m_ref[...]
    o_ref[...] = _dot(hi, m) + _dot(mid, m) + _dot(lo, m)


def _suffix_selector(rows, heads):
    src = np.arange(rows * heads)
    dst = np.arange(rows * heads)
    same_head = (src % heads)[:, None] == (dst % heads)[None, :]
    later = (src // heads)[:, None] > (dst // heads)[None, :]
    total = (src % heads)[:, None] == (np.arange(LANES) % heads)[None, :]
    return jnp.asarray(np.concatenate([same_head & later, total], axis=1), dtype=BF16)


def _page_suffix_sums(logf_flat, sel, *, tp):
    P, K = logf_flat.shape
    N = sel.shape[1]
    block_bytes = _nbytes((tp, K), F32) + _nbytes((tp, N), F32) + _nbytes((K, N), BF16)
    return pl.pallas_call(
        _suffix_kernel,
        grid=(P // tp,),
        in_specs=[pl.BlockSpec((tp, K), lambda i: (i, 0)),
                  pl.BlockSpec((K, N), lambda i: (0, 0))],
        out_specs=pl.BlockSpec((tp, N), lambda i: (i, 0)),
        out_shape=jax.ShapeDtypeStruct((P, N), F32),
        compiler_params=_params(("parallel",), block_bytes),
        name="page_suffix_sums",
    )(logf_flat, sel)


def _fox_sample_kernel(pt_ref, q_ref, kn_ref, vn_ref, cn_ref, cnrow_ref, *rest, heads, seq, pages, scale):
    k_refs = rest[:pages]
    v_refs = rest[pages:2 * pages]
    w_refs = rest[2 * pages:3 * pages]
    o_ref, cq_ref, m_ref, l_ref, acc_ref, carry_ref = rest[3 * pages:]
    del pt_ref
    g = pl.program_id(1)
    rows = seq * heads
    hd = q_ref.shape[2]
    n_keys = k_refs[0].shape[2] * heads
    chunks = n_keys // LANES
    q = q_ref[0]

    @pl.when(g == 0)
    def _():
        cn = cn_ref[0]
        sub = lax.broadcasted_iota(jnp.int32, (heads, heads), 0)
        ln = lax.broadcasted_iota(jnp.int32, (heads, heads), 1)
        c_q = jnp.concatenate(
            [jnp.sum(jnp.where(sub == ln, jnp.broadcast_to(cn[t:t + 1, :], (heads, heads)), 0.0),
                     axis=1, keepdims=True) for t in range(seq)], axis=0)
        cq_ref[...] = jnp.broadcast_to(c_q, cq_ref.shape)
        s = _dot_nt(q, kn_ref[0]) * scale + c_q - cnrow_ref[0]
        r_i = lax.broadcasted_iota(jnp.int32, s.shape, 0)
        k_i = lax.broadcasted_iota(jnp.int32, s.shape, 1)
        keep = (r_i % heads == k_i % heads) & (k_i // heads <= r_i // heads)
        s = jnp.where(keep, s, -jnp.inf)
        m0 = jnp.max(s, axis=-1, keepdims=True)
        p = jnp.exp(s - m0)
        m_ref[...] = jnp.broadcast_to(m0, m_ref.shape)
        l_ref[...] = jnp.broadcast_to(jnp.sum(p, axis=-1, keepdims=True), l_ref.shape)
        acc_ref[...] = _dot(p.astype(BF16), vn_ref[0])
        carry_ref[...] = jnp.zeros_like(carry_ref)

    same_head = (lax.broadcasted_iota(jnp.int32, (rows, n_keys), 0) % heads
                 == lax.broadcasted_iota(jnp.int32, (rows, n_keys), 1) % heads)
    c_q = cq_ref[...][:, 0:1]
    m = m_ref[...][:, 0:1]
    l = l_ref[...][:, 0:1]
    acc = acc_ref[...]
    carry = carry_ref[...]
    for i in reversed(range(pages)):
        w = w_refs[i][0, 0]
        bias = jnp.concatenate([w[c:c + 1, :] + carry for c in range(chunks)], axis=1)
        carry = carry + w[chunks:chunks + 1, :]
        k2 = k_refs[i][0, 0].reshape(n_keys, hd).astype(BF16)
        s = _dot_nt(q, k2) * scale + (c_q + bias)
        s = jnp.where(same_head, s, -jnp.inf)
        m_new = jnp.maximum(m, jnp.max(s, axis=-1, keepdims=True))
        p = jnp.exp(s - m_new)
        a = jnp.exp(m - m_new)
        l = a * l + jnp.sum(p, axis=-1, keepdims=True)
        v2 = v_refs[i][0, 0].reshape(n_keys, hd).astype(BF16)
        acc = a * acc + _dot(p.astype(BF16), v2)
        m = m_new
    m_ref[...] = jnp.broadcast_to(m, m_ref.shape)
    l_ref[...] = jnp.broadcast_to(l, l_ref.shape)
    acc_ref[...] = acc
    carry_ref[...] = carry

    @pl.when(g == pl.num_programs(1) - 1)
    def _():
        o_ref[0] = (acc / l).astype(o_ref.dtype)


def _fox_sample(page_table, q2, kn2, vn2, cn3, cn_row, cache_k, cache_v, w_sfx, layer, *, pages):
    DB, rows, hd = q2.shape
    _, L, H = cn3.shape
    n_pages = page_table.shape[1]
    page = cache_k.shape[2]
    wrows = w_sfx.shape[2]
    assert n_pages % pages == 0 and page * H == (wrows - 1) * LANES and LANES % H == 0
    groups = n_pages // pages
    scale = float(hd ** -0.5)

    def page_map(i, trailing):
        return lambda b, g, pt: (layer, pt[b, (groups - 1 - g) * pages + i]) + (0,) * trailing

    kv_spec = [pl.BlockSpec((1, 1, page, H, hd), page_map(i, 3)) for i in range(pages)]
    w_spec = [pl.BlockSpec((1, 1, wrows, LANES), page_map(i, 2)) for i in range(pages)]
    per_batch = lambda b, g, pt: (b, 0, 0)
    grid_spec = pltpu.PrefetchScalarGridSpec(
        num_scalar_prefetch=1,
        grid=(DB, groups),
        in_specs=[pl.BlockSpec((1, rows, hd), per_batch),
                  pl.BlockSpec((1, rows, hd), per_batch),
                  pl.BlockSpec((1, rows, hd), per_batch),
                  pl.BlockSpec((1, L, H), per_batch),
                  pl.BlockSpec((1, 1, rows), per_batch)] + kv_spec + kv_spec + w_spec,
        out_specs=pl.BlockSpec((1, rows, hd), per_batch),
        scratch_shapes=[pltpu.VMEM((rows, LANES), F32),
                        pltpu.VMEM((rows, LANES), F32),
                        pltpu.VMEM((rows, LANES), F32),
                        pltpu.VMEM((rows, hd), F32),
                        pltpu.VMEM((1, LANES), F32)])
    block_bytes = 2 * pages * _nbytes((page, H, hd), F32) + pages * _nbytes((16, LANES), F32)
    return pl.pallas_call(
        functools.partial(_fox_sample_kernel, heads=H, seq=L, pages=pages, scale=scale),
        grid_spec=grid_spec,
        out_shape=jax.ShapeDtypeStruct((DB, rows, hd), BF16),
        compiler_params=_params(("parallel", "arbitrary"), block_bytes),
        name="fox_sample",
    )(page_table, q2, kn2, vn2, cn3, cn_row, *([cache_k] * pages), *([cache_v] * pages), *([w_sfx] * pages))


def _mix_out_kernel(ret_ref, fox_ref, wr_ref, wf_ref, x_ref, g_ref, b_ref, o_ref, *, alpha):
    y = alpha * x_ref[...] + _dot(ret_ref[...], wr_ref[...]) + _dot(fox_ref[...], wf_ref[...])
    o_ref[...] = _layer_norm(y, g_ref[...], b_ref[...])


def _mix_out(ret, fox, w_ret, w_fox, x, g, b, *, alpha, tm):
    T, D = x.shape
    Wr = ret.shape[1]
    Wf = fox.shape[1]
    block_bytes = (_nbytes((tm, Wr + Wf), BF16) + _nbytes((Wr + Wf, D), BF16) + 2 * _nbytes((tm, D), F32))
    return pl.pallas_call(
        functools.partial(_mix_out_kernel, alpha=alpha),
        grid=(T // tm,),
        in_specs=[pl.BlockSpec((tm, Wr), lambda i: (i, 0)),
                  pl.BlockSpec((tm, Wf), lambda i: (i, 0)),
                  pl.BlockSpec((Wr, D), lambda i: (0, 0)),
                  pl.BlockSpec((Wf, D), lambda i: (0, 0)),
                  pl.BlockSpec((tm, D), lambda i: (i, 0)),
                  pl.BlockSpec((1, D), lambda i: (0, 0)),
                  pl.BlockSpec((1, D), lambda i: (0, 0))],
        out_specs=pl.BlockSpec((tm, D), lambda i: (i, 0)),
        out_shape=jax.ShapeDtypeStruct((T, D), F32),
        compiler_params=_params(("parallel",), block_bytes),
        name="mix_out",
    )(ret, fox, w_ret, w_fox, x, g, b)


def _rope_tables(pos, half):
    inv = ROPE_BASE ** (-jnp.arange(half, dtype=F32) / half)
    ang = pos.astype(F32)[:, None] * inv[None, :]
    return jnp.cos(ang), jnp.sin(ang)


def kernel(x_prompt, x_sample, cache_k, cache_v, cache_logf, state_ret, page_table, ln1_g, ln1_b, w_ffn1_gate, w_ffn1_up, w_ffn1_down, ln2_g, ln2_b, w_in, b_f, w_out, ln3_g, ln3_b, w_ffn2_gate, w_ffn2_up, w_ffn2_down):
    B, S, D = x_prompt.shape
    DB, L, _ = x_sample.shape
    depth, n_pool, page, H, hd_fox = cache_k.shape
    assert H == FOX_HEADS
    n_pages = page_table.shape[1]
    past_len = n_pages * page
    ret_w = RET_HEADS * state_ret.shape[-1]
    fox_w = H * hd_fox
    ret_hd = ret_w // RET_HEADS
    alpha = float((2 * depth) ** 0.25)
    Tp, Ts = B * S, DB * L

    tm_p = 512
    tf = 512
    tm_proj = 1024
    tq = 256
    pages_per_step = 8

    cos_p, sin_p = _rope_tables(jnp.arange(S), ret_hd // 2)
    cos_s, sin_s = _rope_tables(jnp.tile(past_len + jnp.arange(L), DB), ret_hd // 2)

    xp = x_prompt.reshape(Tp, D)
    xs = x_sample.reshape(Ts, D)
    w_sfx = _page_suffix_sums(cache_logf.reshape(depth * n_pool, page * H), _suffix_selector(page, H), tp=512)
    w_sfx = w_sfx.reshape(depth, n_pool, -1, LANES)

    outs = {k: [] for k in ("rs_p", "k_p", "v_p", "lf_p", "rs_s", "k_s", "v_s", "lf_s")}
    row = lambda v: v.reshape(1, -1)
    for l in range(depth):
        wg1, wu1, wd1 = (w.astype(BF16) for w in (w_ffn1_gate[l], w_ffn1_up[l], w_ffn1_down[l]))
        wg2, wu2, wd2 = (w.astype(BF16) for w in (w_ffn2_gate[l], w_ffn2_up[l], w_ffn2_down[l]))
        win = w_in[l].astype(BF16)
        w_rope = win[:, :2 * ret_w]
        w_plain = win[:, 2 * ret_w:4 * ret_w + fox_w]
        w_k = win[:, 4 * ret_w + fox_w:4 * ret_w + 2 * fox_w]
        w_v = win[:, 4 * ret_w + 2 * fox_w:4 * ret_w + 3 * fox_w]
        w_f = win[:, 4 * ret_w + 3 * fox_w:]
        bf_row = b_f[l].astype(F32).reshape(1, H)
        bf_col = b_f[l].astype(F32).reshape(H, 1)
        wo = w_out[l].astype(BF16)
        w_o_ret, w_o_fox = wo[:ret_w], wo[ret_w:]
        q_lo = 2 * ret_w

        xp, xpb = _ffn_sublayer(xp, wg1, wu1, wd1, row(ln1_g[l]), row(ln1_b[l]),
                                alpha=alpha, tm=tm_p, tf=tf, emit_bf16=True)
        xs, xsb = _ffn_sublayer(xs, wg1, wu1, wd1, row(ln1_g[l]), row(ln1_b[l]),
                                alpha=alpha, tm=Ts, tf=tf, emit_bf16=True)

        zr = _proj_rope(xpb, w_rope, cos_p, sin_p, tm=tm_proj, heads=RET_HEADS, k_scale=float(ret_hd ** -0.5))
        zp = _proj(xpb, w_plain, tm=tm_proj, tn=ret_w)
        fk, fkb = _proj_rows(xpb, w_k, tm=tm_proj, heads=H)
        fv, fvb = _proj_rows(xpb, w_v, tm=tm_proj, heads=H)
        logf, c, ct = _fgate_prompt(xpb.reshape(B, S, D), w_f, w_f.T, bf_row, bf_col)
        ret_o, s_fin = _ret_prompt(zr, zp, B=B, S=S)
        fox_o = _fox_prompt(zp.reshape(B, S, -1), fkb.reshape(B, S, fox_w), fvb.reshape(B, S, fox_w), c, ct,
                            tq=tq, q_col=q_lo // fox_w)
        xp = _mix_out(ret_o, fox_o.reshape(Tp, fox_w), w_o_ret, w_o_fox, xp, row(ln2_g[l]), row(ln2_b[l]),
                      alpha=alpha, tm=tm_p)
        outs["rs_p"].append(s_fin)
        outs["k_p"].append(fk.reshape(B, S, H, hd_fox))
        outs["v_p"].append(fv.reshape(B, S, H, hd_fox))
        outs["lf_p"].append(logf)

        zr = _proj_rope(xsb, w_rope, cos_s, sin_s, tm=Ts, heads=RET_HEADS, k_scale=float(ret_hd ** -0.5))
        zp = _proj(xsb, w_plain, tm=Ts, tn=ret_w)
        fk, fkb = _proj_rows(xsb, w_k, tm=Ts, heads=H)
        fv, fvb = _proj_rows(xsb, w_v, tm=Ts, heads=H)
        logf, cn = _fgate_sample(xsb, w_f, bf_row, seq=L)
        ret_o, s_new = _ret_sample(zr.reshape(DB, L, -1), zp.reshape(DB, L, -1), state_ret, l)
        fox_o = _fox_sample(page_table,
                            zp[:, q_lo:q_lo + fox_w].reshape(DB, L * H, hd_fox),
                            fkb.reshape(DB, L * H, hd_fox), fvb.reshape(DB, L * H, hd_fox),
                            cn.reshape(DB, L, H), cn.reshape(DB, 1, L * H),
                            cache_k, cache_v, w_sfx, l, pages=pages_per_step)
        xs = _mix_out(ret_o.reshape(Ts, ret_w), fox_o.reshape(Ts, fox_w), w_o_ret, w_o_fox, xs,
                      row(ln2_g[l]), row(ln2_b[l]), alpha=alpha, tm=Ts)
        outs["rs_s"].append(s_new)
        outs["k_s"].append(fk.reshape(DB, L, H, hd_fox))
        outs["v_s"].append(fv.reshape(DB, L, H, hd_fox))
        outs["lf_s"].append(logf.reshape(DB, L, H))

        (xp,) = _ffn_sublayer(xp, wg2, wu2, wd2, row(ln3_g[l]), row(ln3_b[l]),
                              alpha=alpha, tm=tm_p, tf=tf, emit_bf16=False)
        (xs,) = _ffn_sublayer(xs, wg2, wu2, wd2, row(ln3_g[l]), row(ln3_b[l]),
                              alpha=alpha, tm=Ts, tf=tf, emit_bf16=False)

    stack = lambda k: outs[k][0][None] if depth == 1 else jnp.stack(outs[k])
    return (xp.reshape(B, S, D), xs.reshape(DB, L, D),
            stack("rs_p"), stack("k_p"), stack("v_p"), stack("lf_p"),
            stack("rs_s"), stack("k_s"), stack("v_s"), stack("lf_s"))
```

```python
import functools

import numpy as np
import jax
import jax.numpy as jnp
from jax import lax
from jax.experimental import pallas as pl
from jax.experimental.pallas import tpu as pltpu

F32 = jnp.float32
BF16 = jnp.bfloat16

RET_HEADS = 4
FOX_HEADS = 8
RET_CHUNK = 128
ROPE_BASE = 10000.0
LN_EPS = 1e-5
GN_EPS = 1e-6
LOG2E = float(np.log2(np.e))

LANES = 128
V7X_VMEM_BYTES = 64 * 1024 * 1024
COMPILER_SCRATCH_BYTES = 12 * 1024 * 1024


def _nbytes(shape, dtype):
    return int(np.prod(shape)) * jnp.dtype(dtype).itemsize


def _params(semantics, block_bytes, scratch_bytes=0):
    need = 2 * block_bytes + scratch_bytes + COMPILER_SCRATCH_BYTES
    limit = int(min(need, V7X_VMEM_BYTES - 4 * 1024 * 1024))
    return pltpu.CompilerParams(dimension_semantics=semantics, vmem_limit_bytes=limit)


def _layer_norm(y, g, b):
    mu = jnp.mean(y, axis=-1, keepdims=True)
    yc = y - mu
    var = jnp.mean(yc * yc, axis=-1, keepdims=True)
    return yc * lax.rsqrt(var + LN_EPS) * g + b


def _silu(x):
    return x * jax.nn.sigmoid(x)


def _log_sigmoid(x):
    return jnp.minimum(x, 0.0) - jnp.log1p(jnp.exp(-jnp.abs(x)))


def _dot(a, b):
    return jnp.dot(a, b, preferred_element_type=F32)


def _dot_nt(a, b):
    return lax.dot_general(a, b, (((1,), (1,)), ((), ())), preferred_element_type=F32)


def _dot_tn(a, b):
    return lax.dot_general(a, b, (((0,), (0,)), ((), ())), preferred_element_type=F32)


def _cumsum(x, axis, period=None):
    n = x.shape[axis]
    idx = lax.broadcasted_iota(jnp.int32, x.shape, axis)
    if period is not None:
        idx = idx % period
        n = period
    shift = 1
    while shift < n:
        x = x + jnp.where(idx >= shift, pltpu.roll(x, shift, axis), 0.0)
        shift *= 2
    return x


def _ffn_kernel(x_ref, wg_ref, wu_ref, wd_ref, g_ref, b_ref, *rest, alpha, emit_bf16):
    if emit_bf16:
        o_ref, ob_ref, xb_ref, acc_ref = rest
    else:
        o_ref, xb_ref, acc_ref = rest
    f = pl.program_id(1)

    @pl.when(f == 0)
    def _():
        xb_ref[...] = x_ref[...].astype(BF16)
        acc_ref[...] = jnp.zeros_like(acc_ref)

    xb = xb_ref[...]
    hidden = _silu(_dot(xb, wg_ref[...])) * _dot(xb, wu_ref[...])
    acc_ref[...] += _dot(hidden.astype(BF16), wd_ref[...])

    @pl.when(f == pl.num_programs(1) - 1)
    def _():
        y = _layer_norm(alpha * x_ref[...] + 0.5 * acc_ref[...], g_ref[...], b_ref[...])
        o_ref[...] = y
        if emit_bf16:
            ob_ref[...] = y.astype(BF16)


def _ffn_sublayer(x, wg, wu, wd, g, b, *, alpha, tm, tf, emit_bf16):
    T, D = x.shape
    F = wg.shape[1]
    assert T % tm == 0 and F % tf == 0
    out_shape = [jax.ShapeDtypeStruct((T, D), F32)]
    out_specs = [pl.BlockSpec((tm, D), lambda i, f: (i, 0))]
    if emit_bf16:
        out_shape.append(jax.ShapeDtypeStruct((T, D), BF16))
        out_specs.append(pl.BlockSpec((tm, D), lambda i, f: (i, 0)))
    block_bytes = (2 * _nbytes((tm, D), F32) + int(emit_bf16) * _nbytes((tm, D), BF16)
                   + 3 * _nbytes((D, tf), BF16))
    scratch_bytes = _nbytes((tm, D), BF16) + _nbytes((tm, D), F32)
    return pl.pallas_call(
        functools.partial(_ffn_kernel, alpha=alpha, emit_bf16=emit_bf16),
        grid=(T // tm, F // tf),
        in_specs=[pl.BlockSpec((tm, D), lambda i, f: (i, 0)),
                  pl.BlockSpec((D, tf), lambda i, f: (0, f)),
                  pl.BlockSpec((D, tf), lambda i, f: (0, f)),
                  pl.BlockSpec((tf, D), lambda i, f: (f, 0)),
                  pl.BlockSpec((1, D), lambda i, f: (0, 0)),
                  pl.BlockSpec((1, D), lambda i, f: (0, 0))],
        out_specs=out_specs,
        out_shape=out_shape,
        scratch_shapes=[pltpu.VMEM((tm, D), BF16), pltpu.VMEM((tm, D), F32)],
        compiler_params=_params(("parallel", "arbitrary"), block_bytes, scratch_bytes),
        name="ffn_sublayer",
    )(x, wg, wu, wd, g, b)


def _proj_rope_kernel(h_ref, w_ref, cos_ref, sin_ref, o_ref, *, heads, k_scale):
    scale = jnp.where(pl.program_id(0) == 0, 1.0, k_scale).astype(F32)
    z = _dot(h_ref[...], w_ref[...])
    cos = cos_ref[...]
    sin = sin_ref[...]
    hd = z.shape[1] // heads
    half = hd // 2
    for h in range(heads):
        x1 = z[:, h * hd:h * hd + half]
        x2 = z[:, h * hd + half:(h + 1) * hd]
        o_ref[:, h * hd:h * hd + half] = ((x1 * cos - x2 * sin) * scale).astype(o_ref.dtype)
        o_ref[:, h * hd + half:(h + 1) * hd] = ((x1 * sin + x2 * cos) * scale).astype(o_ref.dtype)


def _proj_rope(hb, w, cos, sin, *, tm, tn, heads, k_scale):
    T, D = hb.shape
    N = 2 * tn
    half = cos.shape[1]
    pos_blocks = cos.shape[0] // tm
    block_bytes = (_nbytes((tm, D), BF16) + _nbytes((D, tn), BF16) + 2 * _nbytes((tm, half), F32)
                   + _nbytes((tm, tn), BF16))
    return pl.pallas_call(
        functools.partial(_proj_rope_kernel, heads=heads, k_scale=k_scale),
        grid=(2, T // tm),
        in_specs=[pl.BlockSpec((tm, D), lambda j, i: (i, 0)),
                  pl.BlockSpec((D, tn), lambda j, i: (0, j)),
                  pl.BlockSpec((tm, half), lambda j, i: (i % pos_blocks, 0)),
                  pl.BlockSpec((tm, half), lambda j, i: (i % pos_blocks, 0))],
        out_specs=pl.BlockSpec((tm, tn), lambda j, i: (i, j)),
        out_shape=jax.ShapeDtypeStruct((T, N), BF16),
        compiler_params=_params(("parallel", "parallel"), block_bytes),
        name="proj_rope",
    )(hb, w, cos, sin)


def _proj_kernel(h_ref, w_ref, o_ref):
    o_ref[...] = _dot(h_ref[...], w_ref[...]).astype(o_ref.dtype)


def _proj(hb, w, *, tm, tn, col0, n_blocks):
    T, D = hb.shape
    N = n_blocks * tn
    block_bytes = _nbytes((tm, D), BF16) + _nbytes((D, tn), BF16) + _nbytes((tm, tn), BF16)
    return pl.pallas_call(
        _proj_kernel,
        grid=(n_blocks, T // tm),
        in_specs=[pl.BlockSpec((tm, D), lambda j, i: (i, 0)),
                  pl.BlockSpec((D, tn), lambda j, i: (0, col0 + j))],
        out_specs=pl.BlockSpec((tm, tn), lambda j, i: (i, j)),
        out_shape=jax.ShapeDtypeStruct((T, N), BF16),
        compiler_params=_params(("parallel", "parallel"), block_bytes),
        name="proj",
    )(hb, w)


def _proj_rows_kernel(h_ref, w_ref, rows_ref, ob_ref, *, heads):
    z = _dot(h_ref[...], w_ref[...])
    ob_ref[...] = z.astype(ob_ref.dtype)
    hd = z.shape[1] // heads
    for h in range(heads):
        rows_ref[:, h, :] = z[:, h * hd:(h + 1) * hd]


def _proj_rows(hb, w, *, tm, heads, hd, col0):
    T, D = hb.shape
    N = heads * hd
    block_bytes = (_nbytes((tm, D), BF16) + _nbytes((D, N), BF16) + _nbytes((tm, N), F32)
                   + _nbytes((tm, N), BF16))
    return pl.pallas_call(
        functools.partial(_proj_rows_kernel, heads=heads),
        grid=(T // tm,),
        in_specs=[pl.BlockSpec((tm, D), lambda i: (i, 0)),
                  pl.BlockSpec((D, N), lambda i: (0, col0))],
        out_specs=[pl.BlockSpec((tm, heads, hd), lambda i: (i, 0, 0)),
                   pl.BlockSpec((tm, N), lambda i: (i, 0))],
        out_shape=[jax.ShapeDtypeStruct((T, heads, hd), F32),
                   jax.ShapeDtypeStruct((T, N), BF16)],
        compiler_params=_params(("parallel",), block_bytes),
        name="proj_rows",
    )(hb, w)


def _fgate_prompt_kernel(h_ref, wf_ref, wft_ref, brow_ref, bcol_ref, logf_ref, c_ref, ct_ref):
    h = h_ref[0]
    logf = _log_sigmoid(_dot(h, wf_ref[...]) + brow_ref[...])
    logf_ref[0] = logf
    c_ref[0] = _cumsum(logf, 0)
    logf_t = _log_sigmoid(_dot_nt(wft_ref[...], h) + bcol_ref[...])
    ct_ref[0] = _cumsum(logf_t, 1)


def _fgate_prompt(hb3, wf, wft, brow, bcol):
    B, S, D = hb3.shape
    H = wf.shape[1]
    block_bytes = _nbytes((S, D), BF16) + 3 * _nbytes((S, LANES), F32)
    return pl.pallas_call(
        _fgate_prompt_kernel,
        grid=(B,),
        in_specs=[pl.BlockSpec((1, S, D), lambda b: (b, 0, 0)),
                  pl.BlockSpec((D, H), lambda b: (0, 0)),
                  pl.BlockSpec((H, D), lambda b: (0, 0)),
                  pl.BlockSpec((1, H), lambda b: (0, 0)),
                  pl.BlockSpec((H, 1), lambda b: (0, 0))],
        out_specs=[pl.BlockSpec((1, S, H), lambda b: (b, 0, 0)),
                   pl.BlockSpec((1, S, H), lambda b: (b, 0, 0)),
                   pl.BlockSpec((1, H, S), lambda b: (b, 0, 0))],
        out_shape=[jax.ShapeDtypeStruct((B, S, H), F32),
                   jax.ShapeDtypeStruct((B, S, H), F32),
                   jax.ShapeDtypeStruct((B, H, S), F32)],
        compiler_params=_params(("parallel",), block_bytes),
        name="fgate_prompt",
    )(hb3, wf, wft, brow, bcol)


def _fgate_sample_kernel(h_ref, wf_ref, brow_ref, logf_ref, cn_ref, *, seq):
    logf = _log_sigmoid(_dot(h_ref[...], wf_ref[...]) + brow_ref[...])
    logf_ref[...] = logf
    cn_ref[...] = _cumsum(logf, 0, period=seq)


def _fgate_sample(hb, wf, brow, *, seq):
    T, _ = hb.shape
    H = wf.shape[1]
    return pl.pallas_call(
        functools.partial(_fgate_sample_kernel, seq=seq),
        out_shape=[jax.ShapeDtypeStruct((T, H), F32), jax.ShapeDtypeStruct((T, H), F32)],
        name="fgate_sample",
    )(hb, wf, brow)


def _ret_log_gamma(h):
    return jnp.float32(np.log1p(-np.exp2(-5.0 - h)))


def _ret_decays(log_g, C):
    row = lax.broadcasted_iota(jnp.int32, (C, C), 0)
    col = lax.broadcasted_iota(jnp.int32, (C, C), 1)
    rel = (row - col).astype(F32)
    inner = jnp.where(rel >= 0, jnp.exp(log_g * jnp.maximum(rel, 0.0)), 0.0)
    idx = lax.broadcasted_iota(jnp.int32, (C, 1), 0).astype(F32)
    q_decay = jnp.exp(log_g * (idx + 1.0))
    k_decay = jnp.exp(log_g * (C - 1.0 - idx))
    chunk_decay = jnp.exp(log_g * jnp.full((1, 1), float(C), F32))
    return inner, q_decay, k_decay, chunk_decay


def _ret_chunk(q, k, v, g, state, decays):
    inner, q_decay, k_decay, chunk_decay = decays
    att = _dot_nt(q, k) * inner
    o = _dot(att.astype(BF16), v) + _dot(q, state.astype(BF16)) * q_decay
    kd = (k.astype(F32) * k_decay).astype(BF16)
    new_state = chunk_decay * state + _dot_tn(kd, v)
    rn = o * lax.rsqrt(jnp.mean(o * o, axis=-1, keepdims=True) + GN_EPS)
    return (rn * _silu(g.astype(F32))).astype(BF16), new_state


def _ret_prompt_kernel(q_ref, k_ref, v_ref, g_ref, o_ref, s_ref, *, chunk):
    rows, W = q_ref.shape
    hd = W // RET_HEADS
    decays = [_ret_decays(_ret_log_gamma(h), chunk) for h in range(RET_HEADS)]

    @pl.when(pl.program_id(1) == 0)
    def _():
        s_ref[...] = jnp.zeros_like(s_ref)

    def body(c, carry):
        sl = pl.ds(pl.multiple_of(c * chunk, chunk), chunk)
        for h in range(RET_HEADS):
            cols = slice(h * hd, (h + 1) * hd)
            out, new_state = _ret_chunk(q_ref[sl, cols], k_ref[sl, cols], v_ref[sl, cols], g_ref[sl, cols],
                                        s_ref[0, h], decays[h])
            o_ref[sl, cols] = out
            s_ref[0, h] = new_state
        return carry

    lax.fori_loop(0, rows // chunk, body, 0)


def _ret_prompt(zr, zp, *, B, S, seq_blocks):
    W = zr.shape[1] // 2
    hd = W // RET_HEADS
    chunk = RET_CHUNK if S % RET_CHUNK == 0 else S
    rows = S // seq_blocks
    assert S % seq_blocks == 0 and rows % chunk == 0
    block_bytes = 5 * _nbytes((rows, W), BF16) + _nbytes((RET_HEADS, hd, hd), F32)
    tokens = lambda col: (lambda b, i: (b * seq_blocks + i, col))
    return pl.pallas_call(
        functools.partial(_ret_prompt_kernel, chunk=chunk),
        grid=(B, seq_blocks),
        in_specs=[pl.BlockSpec((rows, W), tokens(0)),
                  pl.BlockSpec((rows, W), tokens(1)),
                  pl.BlockSpec((rows, W), tokens(0)),
                  pl.BlockSpec((rows, W), tokens(1))],
        out_specs=[pl.BlockSpec((rows, W), tokens(0)),
                   pl.BlockSpec((1, RET_HEADS, hd, hd), lambda b, i: (b, 0, 0, 0))],
        out_shape=[jax.ShapeDtypeStruct((B * S, W), BF16),
                   jax.ShapeDtypeStruct((B, RET_HEADS, hd, hd), F32)],
        compiler_params=_params(("parallel", "arbitrary"), block_bytes),
        name="retention_prompt",
    )(zr, zr, zp, zp)


def _ret_sample_kernel(zr_ref, zp_ref, s0_ref, o_ref, s_ref, *, width):
    L = zr_ref.shape[1]
    hd = width // RET_HEADS
    for h in range(RET_HEADS):
        decays = _ret_decays(_ret_log_gamma(h), L)
        q = zr_ref[0, :, h * hd:(h + 1) * hd]
        k = zr_ref[0, :, width + h * hd:width + (h + 1) * hd]
        v = zp_ref[0, :, h * hd:(h + 1) * hd]
        g = zp_ref[0, :, width + h * hd:width + (h + 1) * hd]
        out, new_state = _ret_chunk(q, k, v, g, s0_ref[0, 0, h], decays)
        o_ref[0, :, h * hd:(h + 1) * hd] = out
        s_ref[0, h] = new_state


def _ret_sample(zr3, zp3, state, layer):
    DB, L, W2 = zr3.shape
    W = W2 // 2
    hd = W // RET_HEADS
    NP = zp3.shape[2]
    block_bytes = _nbytes((16, W2 + NP + W), BF16) + 2 * _nbytes((RET_HEADS, hd, hd), F32)
    return pl.pallas_call(
        functools.partial(_ret_sample_kernel, width=W),
        grid=(DB,),
        in_specs=[pl.BlockSpec((1, L, W2), lambda b: (b, 0, 0)),
                  pl.BlockSpec((1, L, NP), lambda b: (b, 0, 0)),
                  pl.BlockSpec((1, 1, RET_HEADS, hd, hd), lambda b: (layer, b, 0, 0, 0))],
        out_specs=[pl.BlockSpec((1, L, W), lambda b: (b, 0, 0)),
                   pl.BlockSpec((1, RET_HEADS, hd, hd), lambda b: (b, 0, 0, 0))],
        out_shape=[jax.ShapeDtypeStruct((DB, L, W), BF16),
                   jax.ShapeDtypeStruct((DB, RET_HEADS, hd, hd), F32)],
        compiler_params=_params(("parallel",), block_bytes),
        name="retention_sample",
    )(zr3, zp3, state)


def _fox_prompt_kernel(q_ref, k_ref, v_ref, c_ref, ct_ref, o_ref, qt_ref, vt_ref, acc_ref, m_ref, l_ref,
                       u_ref, p_ref, *, heads, scale):
    tq = q_ref.shape[1]
    S = k_ref.shape[1]
    hd = q_ref.shape[2] // heads
    qi = pl.program_id(1)
    a_scale = scale * LOG2E

    def transposed(x):
        return x.astype(F32).T.astype(BF16)

    @pl.when(qi == 0)
    def _():
        def body(j, carry):
            ks = pl.ds(pl.multiple_of(j * tq, tq), tq)
            for h in range(heads):
                vt_ref[h * hd:(h + 1) * hd, ks] = transposed(v_ref[0, ks, h * hd:(h + 1) * hd])
            return carry
        lax.fori_loop(0, S // tq, body, 0)

    for h in range(heads):
        qt_ref[h * hd:(h + 1) * hd, :] = transposed(q_ref[0, :, h * hd:(h + 1) * hd])
    qs = pl.ds(pl.multiple_of(qi * tq, tq), tq)
    c_q = ct_ref[0, :, qs] * LOG2E
    key = lax.broadcasted_iota(jnp.int32, (tq, tq), 0)
    qry = lax.broadcasted_iota(jnp.int32, (tq, tq), 1)
    causal = key <= qry

    def block(j, first):
        ks = pl.ds(pl.multiple_of(j * tq, tq), tq)
        c_k = c_ref[0, ks, :] * LOG2E
        maxes = []
        for h in range(heads):
            rows = slice(h * hd, (h + 1) * hd)
            u = _dot(k_ref[0, ks, rows], qt_ref[rows, :]) * a_scale - c_k[:, h:h + 1]
            if first:
                u = jnp.where(causal, u, -jnp.inf)
            u_ref[h] = u
            maxes.append(jnp.max(u, axis=0, keepdims=True))
        rescale = []
        for h in range(heads):
            cq = c_q[h:h + 1, :]
            mx = maxes[h] + cq
            if first:
                m_new = mx
            else:
                m_old = m_ref[h:h + 1, :]
                m_new = jnp.maximum(m_old, mx)
            p = jnp.exp2(u_ref[h] + (cq - m_new))
            ps = jnp.sum(p, axis=0, keepdims=True)
            p_ref[h] = p.astype(BF16)
            if first:
                l_ref[h:h + 1, :] = ps
                rescale.append(None)
            else:
                a = jnp.exp2(m_old - m_new)
                l_ref[h:h + 1, :] = a * l_ref[h:h + 1, :] + ps
                rescale.append(a)
            m_ref[h:h + 1, :] = m_new
        for h in range(heads):
            rows = slice(h * hd, (h + 1) * hd)
            pv = _dot(vt_ref[rows, ks], p_ref[h])
            if first:
                acc_ref[rows, :] = pv
            else:
                acc_ref[rows, :] = rescale[h] * acc_ref[rows, :] + pv

    block(qi, True)

    def off_diagonal(j, carry):
        block(j, False)
        return carry

    lax.fori_loop(0, qi, off_diagonal, 0)
    for h in range(heads):
        rows = slice(h * hd, (h + 1) * hd)
        o_ref[0, :, rows] = (acc_ref[rows, :] / l_ref[h:h + 1, :]).T.astype(o_ref.dtype)


def _fox_prompt(fq, fkb, fvb, c, ct, *, tq, q_col):
    B, S, H = c.shape
    W = fkb.shape[2]
    scale = float((W // H) ** -0.5)
    block_bytes = (2 * _nbytes((tq, W), BF16) + 2 * _nbytes((S, W), BF16)
                   + _nbytes((S, LANES), F32) + _nbytes((H, S), F32))
    scratch_bytes = (_nbytes((W, tq), BF16) + _nbytes((W, S), BF16) + _nbytes((W, tq), F32)
                     + _nbytes((H, tq, tq), F32) + _nbytes((H, tq, tq), BF16))
    return pl.pallas_call(
        functools.partial(_fox_prompt_kernel, heads=H, scale=scale),
        grid=(B, S // tq),
        in_specs=[pl.BlockSpec((1, tq, W), lambda b, i: (b, i, q_col)),
                  pl.BlockSpec((1, S, W), lambda b, i: (b, 0, 0)),
                  pl.BlockSpec((1, S, W), lambda b, i: (b, 0, 0)),
                  pl.BlockSpec((1, S, H), lambda b, i: (b, 0, 0)),
                  pl.BlockSpec((1, H, S), lambda b, i: (b, 0, 0))],
        out_specs=pl.BlockSpec((1, tq, W), lambda b, i: (b, i, 0)),
        out_shape=jax.ShapeDtypeStruct((B, S, W), BF16),
        scratch_shapes=[pltpu.VMEM((W, tq), BF16),
                        pltpu.VMEM((W, S), BF16),
                        pltpu.VMEM((W, tq), F32),
                        pltpu.VMEM((H, tq), F32),
                        pltpu.VMEM((H, tq), F32),
                        pltpu.VMEM((H, tq, tq), F32),
                        pltpu.VMEM((H, tq, tq), BF16)],
        compiler_params=_params(("parallel", "arbitrary"), block_bytes, scratch_bytes),
        name="fox_prompt",
    )(fq, fkb, fvb, c, ct)


def _suffix_kernel(x_ref, m_ref, o_ref):
    x = x_ref[...]
    hi = x.astype(BF16)
    r1 = x - hi.astype(F32)
    mid = r1.astype(BF16)
    lo = (r1 - mid.astype(F32)).astype(BF16)
    m = m_ref[...]
    o_ref[...] = _dot(hi, m) + _dot(mid, m) + _dot(lo, m)


def _suffix_selector(rows, heads):
    src = np.arange(rows * heads)
    dst = np.arange(rows * heads)
    same_head = (src % heads)[:, None] == (dst % heads)[None, :]
    later = (src // heads)[:, None] > (dst // heads)[None, :]
    total = (src % heads)[:, None] == (np.arange(LANES) % heads)[None, :]
    return jnp.asarray(np.concatenate([same_head & later, total], axis=1), dtype=BF16)


def _page_suffix_sums(logf_flat, sel, *, tp):
    P, K = logf_flat.shape
    N = sel.shape[1]
    block_bytes = _nbytes((tp, K), F32) + _nbytes((tp, N), F32) + _nbytes((K, N), BF16)
    return pl.pallas_call(
        _suffix_kernel,
        grid=(P // tp,),
        in_specs=[pl.BlockSpec((tp, K), lambda i: (i, 0)),
                  pl.BlockSpec((K, N), lambda i: (0, 0))],
        out_specs=pl.BlockSpec((tp, N), lambda i: (i, 0)),
        out_shape=jax.ShapeDtypeStruct((P, N), F32),
        compiler_params=_params(("parallel",), block_bytes),
        name="page_suffix_sums",
    )(logf_flat, sel)


def _fox_sample_kernel(pt_ref, q_ref, kn_ref, vn_ref, cn_ref, cnrow_ref, *rest, heads, seq, pages, scale):
    k_refs = rest[:pages]
    v_refs = rest[pages:2 * pages]
    w_refs = rest[2 * pages:3 * pages]
    o_ref, cq_ref, m_ref, l_ref, acc_ref, carry_ref = rest[3 * pages:]
    del pt_ref
    g = pl.program_id(1)
    rows = seq * heads
    hd = q_ref.shape[2]
    n_keys = k_refs[0].shape[2] * heads
    chunks = n_keys // LANES
    q = q_ref[0]
    a_scale = scale * LOG2E

    @pl.when(g == 0)
    def _():
        cn = cn_ref[0]
        sub = lax.broadcasted_iota(jnp.int32, (heads, heads), 0)
        ln = lax.broadcasted_iota(jnp.int32, (heads, heads), 1)
        c_q = jnp.concatenate(
            [jnp.sum(jnp.where(sub == ln, jnp.broadcast_to(cn[t:t + 1, :], (heads, heads)), 0.0),
                     axis=1, keepdims=True) for t in range(seq)], axis=0)
        c_q = c_q * LOG2E
        cq_ref[...] = jnp.broadcast_to(c_q, cq_ref.shape)
        s = _dot_nt(q, kn_ref[0]) * a_scale - cnrow_ref[0] * LOG2E
        r_i = lax.broadcasted_iota(jnp.int32, s.shape, 0)
        k_i = lax.broadcasted_iota(jnp.int32, s.shape, 1)
        keep = (r_i % heads == k_i % heads) & (k_i // heads <= r_i // heads)
        s = jnp.where(keep, s, -jnp.inf)
        m0 = jnp.max(s, axis=-1, keepdims=True) + c_q
        p = jnp.exp2(s + (c_q - m0))
        m_ref[...] = jnp.broadcast_to(m0, m_ref.shape)
        l_ref[...] = jnp.broadcast_to(jnp.sum(p, axis=-1, keepdims=True), l_ref.shape)
        acc_ref[...] = _dot(p.astype(BF16), vn_ref[0])
        carry_ref[...] = jnp.zeros_like(carry_ref)

    same_head = (lax.broadcasted_iota(jnp.int32, (rows, n_keys), 0) % heads
                 == lax.broadcasted_iota(jnp.int32, (rows, n_keys), 1) % heads)
    c_q = cq_ref[...][:, 0:1]
    m = m_ref[...][:, 0:1]
    l = l_ref[...][:, 0:1]
    acc = acc_ref[...]
    carry = carry_ref[...]
    s_pages = [None] * pages
    for i in reversed(range(pages)):
        w = w_refs[i][0, 0]
        bias = jnp.concatenate([w[c:c + 1, :] + carry for c in range(chunks)], axis=1) * LOG2E
        carry = carry + w[chunks:chunks + 1, :]
        k2 = k_refs[i][0, 0].reshape(n_keys, hd).astype(BF16)
        s_pages[i] = jnp.where(same_head, _dot_nt(q, k2) * a_scale + bias, -jnp.inf)
    s_max = s_pages[0]
    for i in range(1, pages):
        s_max = jnp.maximum(s_max, s_pages[i])
    m_new = jnp.maximum(m, jnp.max(s_max, axis=-1, keepdims=True) + c_q)
    a = jnp.exp2(m - m_new)
    shift = c_q - m_new
    p_sum = jnp.zeros((rows, n_keys), F32)
    acc = a * acc
    for i in range(pages):
        p = jnp.exp2(s_pages[i] + shift)
        p_sum = p_sum + p
        acc = acc + _dot(p.astype(BF16), v_refs[i][0, 0].reshape(n_keys, hd).astype(BF16))
    l = a * l + jnp.sum(p_sum, axis=-1, keepdims=True)
    m_ref[...] = jnp.broadcast_to(m_new, m_ref.shape)
    l_ref[...] = jnp.broadcast_to(l, l_ref.shape)
    acc_ref[...] = acc
    carry_ref[...] = carry

    @pl.when(g == pl.num_programs(1) - 1)
    def _():
        o_ref[0] = (acc / l).astype(o_ref.dtype)


def _fox_sample(page_table, q2, kn2, vn2, cn3, cn_row, cache_k, cache_v, w_sfx, layer, *, pages):
    DB, rows, hd = q2.shape
    _, L, H = cn3.shape
    n_pages = page_table.shape[1]
    page = cache_k.shape[2]
    wrows = w_sfx.shape[2]
    assert n_pages % pages == 0 and page * H == (wrows - 1) * LANES and LANES % H == 0
    groups = n_pages // pages
    scale = float(hd ** -0.5)

    def page_map(i, trailing):
        return lambda b, g, pt: (layer, pt[b, (groups - 1 - g) * pages + i]) + (0,) * trailing

    kv_spec = [pl.BlockSpec((1, 1, page, H, hd), page_map(i, 3)) for i in range(pages)]
    w_spec = [pl.BlockSpec((1, 1, wrows, LANES), page_map(i, 2)) for i in range(pages)]
    per_batch = lambda b, g, pt: (b, 0, 0)
    grid_spec = pltpu.PrefetchScalarGridSpec(
        num_scalar_prefetch=1,
        grid=(DB, groups),
        in_specs=[pl.BlockSpec((1, rows, hd), per_batch),
                  pl.BlockSpec((1, rows, hd), per_batch),
                  pl.BlockSpec((1, rows, hd), per_batch),
                  pl.BlockSpec((1, L, H), per_batch),
                  pl.BlockSpec((1, 1, rows), per_batch)] + kv_spec + kv_spec + w_spec,
        out_specs=pl.BlockSpec((1, rows, hd), per_batch),
        scratch_shapes=[pltpu.VMEM((rows, LANES), F32),
                        pltpu.VMEM((rows, LANES), F32),
                        pltpu.VMEM((rows, LANES), F32),
                        pltpu.VMEM((rows, hd), F32),
                        pltpu.VMEM((1, LANES), F32)])
    block_bytes = 2 * pages * _nbytes((page, H, hd), F32) + pages * _nbytes((16, LANES), F32)
    return pl.pallas_call(
        functools.partial(_fox_sample_kernel, heads=H, seq=L, pages=pages, scale=scale),
        grid_spec=grid_spec,
        out_shape=jax.ShapeDtypeStruct((DB, rows, hd), BF16),
        compiler_params=_params(("parallel", "arbitrary"), block_bytes),
        name="fox_sample",
    )(page_table, q2, kn2, vn2, cn3, cn_row, *([cache_k] * pages), *([cache_v] * pages), *([w_sfx] * pages))


def _mix_out_kernel(ret_ref, fox_ref, wr_ref, wf_ref, x_ref, g_ref, b_ref, o_ref, *, alpha):
    y = alpha * x_ref[...] + _dot(ret_ref[...], wr_ref[...]) + _dot(fox_ref[...], wf_ref[...])
    o_ref[...] = _layer_norm(y, g_ref[...], b_ref[...])


def _mix_out(ret, fox, w_out, x, g, b, *, alpha, tm):
    T, D = x.shape
    Wr = ret.shape[1]
    Wf = fox.shape[1]
    assert Wr == Wf and w_out.shape[0] == Wr + Wf
    block_bytes = (_nbytes((tm, Wr + Wf), BF16) + _nbytes((Wr + Wf, D), BF16) + 2 * _nbytes((tm, D), F32))
    return pl.pallas_call(
        functools.partial(_mix_out_kernel, alpha=alpha),
        grid=(T // tm,),
        in_specs=[pl.BlockSpec((tm, Wr), lambda i: (i, 0)),
                  pl.BlockSpec((tm, Wf), lambda i: (i, 0)),
                  pl.BlockSpec((Wr, D), lambda i: (0, 0)),
                  pl.BlockSpec((Wf, D), lambda i: (1, 0)),
                  pl.BlockSpec((tm, D), lambda i: (i, 0)),
                  pl.BlockSpec((1, D), lambda i: (0, 0)),
                  pl.BlockSpec((1, D), lambda i: (0, 0))],
        out_specs=pl.BlockSpec((tm, D), lambda i: (i, 0)),
        out_shape=jax.ShapeDtypeStruct((T, D), F32),
        compiler_params=_params(("parallel",), block_bytes),
        name="mix_out",
    )(ret, fox, w_out, w_out, x, g, b)


def _rope_tables(pos, half):
    inv = ROPE_BASE ** (-jnp.arange(half, dtype=F32) / half)
    ang = pos.astype(F32)[:, None] * inv[None, :]
    return jnp.cos(ang), jnp.sin(ang)


def kernel(x_prompt, x_sample, cache_k, cache_v, cache_logf, state_ret, page_table, ln1_g, ln1_b, w_ffn1_gate, w_ffn1_up, w_ffn1_down, ln2_g, ln2_b, w_in, b_f, w_out, ln3_g, ln3_b, w_ffn2_gate, w_ffn2_up, w_ffn2_down):
    B, S, D = x_prompt.shape
    DB, L, _ = x_sample.shape
    depth, n_pool, page, H, hd_fox = cache_k.shape
    assert H == FOX_HEADS
    n_pages = page_table.shape[1]
    past_len = n_pages * page
    ret_w = RET_HEADS * state_ret.shape[-1]
    fox_w = H * hd_fox
    ret_hd = ret_w // RET_HEADS
    assert ret_w == fox_w
    cw = ret_w
    alpha = float((2 * depth) ** 0.25)
    Tp, Ts = B * S, DB * L

    tm_p = 512
    tf = 512
    tm_proj = 1024
    tq = 256
    pages_per_step = 8

    cos_p, sin_p = _rope_tables(jnp.arange(S), ret_hd // 2)
    cos_s, sin_s = _rope_tables(jnp.tile(past_len + jnp.arange(L), DB), ret_hd // 2)

    xp = x_prompt.reshape(Tp, D)
    xs = x_sample.reshape(Ts, D)
    w_sfx = _page_suffix_sums(cache_logf.reshape(depth * n_pool, page * H), _suffix_selector(page, H), tp=512)
    w_sfx = w_sfx.reshape(depth, n_pool, -1, LANES)

    outs = {k: [] for k in ("rs_p", "k_p", "v_p", "lf_p", "rs_s", "k_s", "v_s", "lf_s")}
    row = lambda v: v.reshape(1, -1)
    for l in range(depth):
        wg1, wu1, wd1 = (w.astype(BF16) for w in (w_ffn1_gate[l], w_ffn1_up[l], w_ffn1_down[l]))
        wg2, wu2, wd2 = (w.astype(BF16) for w in (w_ffn2_gate[l], w_ffn2_up[l], w_ffn2_down[l]))
        win = w_in[l].astype(BF16)
        w_f = win[:, 7 * cw:]
        bf_row = b_f[l].astype(F32).reshape(1, H)
        bf_col = b_f[l].astype(F32).reshape(H, 1)
        wo = w_out[l].astype(BF16)
        q_lo = 2 * cw

        xp, xpb = _ffn_sublayer(xp, wg1, wu1, wd1, row(ln1_g[l]), row(ln1_b[l]),
                                alpha=alpha, tm=tm_p, tf=tf, emit_bf16=True)
        xs, xsb = _ffn_sublayer(xs, wg1, wu1, wd1, row(ln1_g[l]), row(ln1_b[l]),
                                alpha=alpha, tm=Ts, tf=tf, emit_bf16=True)

        zr = _proj_rope(xpb, win, cos_p, sin_p, tm=tm_proj, tn=cw, heads=RET_HEADS, k_scale=float(ret_hd ** -0.5))
        zp = _proj(xpb, win, tm=tm_proj, tn=cw, col0=2, n_blocks=3)
        fk, fkb = _proj_rows(xpb, win, tm=tm_proj, heads=H, hd=hd_fox, col0=5)
        fv, fvb = _proj_rows(xpb, win, tm=tm_proj, heads=H, hd=hd_fox, col0=6)
        logf, c, ct = _fgate_prompt(xpb.reshape(B, S, D), w_f, w_f.T, bf_row, bf_col)
        ret_o, s_fin = _ret_prompt(zr, zp, B=B, S=S, seq_blocks=2)
        fox_o = _fox_prompt(zp.reshape(B, S, -1), fkb.reshape(B, S, fox_w), fvb.reshape(B, S, fox_w), c, ct,
                            tq=tq, q_col=q_lo // fox_w)
        xp = _mix_out(ret_o, fox_o.reshape(Tp, fox_w), wo, xp, row(ln2_g[l]), row(ln2_b[l]),
                      alpha=alpha, tm=tm_p)
        outs["rs_p"].append(s_fin)
        outs["k_p"].append(fk.reshape(B, S, H, hd_fox))
        outs["v_p"].append(fv.reshape(B, S, H, hd_fox))
        outs["lf_p"].append(logf)

        zr = _proj_rope(xsb, win, cos_s, sin_s, tm=Ts, tn=cw, heads=RET_HEADS, k_scale=float(ret_hd ** -0.5))
        zp = _proj(xsb, win, tm=Ts, tn=cw, col0=2, n_blocks=3)
        fk, fkb = _proj_rows(xsb, win, tm=Ts, heads=H, hd=hd_fox, col0=5)
        fv, fvb = _proj_rows(xsb, win, tm=Ts, heads=H, hd=hd_fox, col0=6)
        logf, cn = _fgate_sample(xsb, w_f, bf_row, seq=L)
        ret_o, s_new = _ret_sample(zr.reshape(DB, L, -1), zp.reshape(DB, L, -1), state_ret, l)
        fox_o = _fox_sample(page_table,
                            zp[:, q_lo:q_lo + fox_w].reshape(DB, L * H, hd_fox),
                            fkb.reshape(DB, L * H, hd_fox), fvb.reshape(DB, L * H, hd_fox),
                            cn.reshape(DB, L, H), cn.reshape(DB, 1, L * H),
                            cache_k, cache_v, w_sfx, l, pages=pages_per_step)
        xs = _mix_out(ret_o.reshape(Ts, ret_w), fox_o.reshape(Ts, fox_w), wo, xs,
                      row(ln2_g[l]), row(ln2_b[l]), alpha=alpha, tm=Ts)
        outs["rs_s"].append(s_new)
        outs["k_s"].append(fk.reshape(DB, L, H, hd_fox))
        outs["v_s"].append(fv.reshape(DB, L, H, hd_fox))
        outs["lf_s"].append(logf.reshape(DB, L, H))

        (xp,) = _ffn_sublayer(xp, wg2, wu2, wd2, row(ln3_g[l]), row(ln3_b[l]),
                              alpha=alpha, tm=tm_p, tf=tf, emit_bf16=False)
        (xs,) = _ffn_sublayer(xs, wg2, wu2, wd2, row(ln3_g[l]), row(ln3_b[l]),
                              alpha=alpha, tm=Ts, tf=tf, emit_bf16=False)

    stack = lambda k: outs[k][0][None] if depth == 1 else jnp.stack(outs[k])
    return (xp.reshape(B, S, D), xs.reshape(DB, L, D),
            stack("rs_p"), stack("k_p"), stack("v_p"), stack("lf_p"),
            stack("rs_s"), stack("k_s"), stack("v_s"), stack("lf_s"))
```

```python
import functools

import numpy as np
import jax
import jax.numpy as jnp
from jax import lax
from jax.experimental import pallas as pl
from jax.experimental.pallas import tpu as pltpu

F32 = jnp.float32
BF16 = jnp.bfloat16

RET_HEADS = 4
FOX_HEADS = 8
RET_CHUNK = 128
ROPE_BASE = 10000.0
LN_EPS = 1e-5
GN_EPS = 1e-6
LOG2E = float(np.log2(np.e))

LANES = 128
V7X_VMEM_BYTES = 64 * 1024 * 1024
MAX_SCOPED_VMEM_BYTES = 58 * 1024 * 1024
COMPILER_SCRATCH_BYTES = 12 * 1024 * 1024


def _nbytes(shape, dtype):
    return int(np.prod(shape)) * jnp.dtype(dtype).itemsize


def _params(semantics, block_bytes, scratch_bytes=0):
    need = 2 * block_bytes + scratch_bytes + COMPILER_SCRATCH_BYTES
    limit = int(min(need, MAX_SCOPED_VMEM_BYTES))
    return pltpu.CompilerParams(dimension_semantics=semantics, vmem_limit_bytes=limit)


def _layer_norm(y, g, b):
    mu = jnp.mean(y, axis=-1, keepdims=True)
    yc = y - mu
    var = jnp.mean(yc * yc, axis=-1, keepdims=True)
    return yc * lax.rsqrt(var + LN_EPS) * g + b


def _silu(x):
    return x * jax.nn.sigmoid(x)


def _log_sigmoid(x):
    return jnp.minimum(x, 0.0) - jnp.log1p(jnp.exp(-jnp.abs(x)))


def _dot(a, b):
    return jnp.dot(a, b, preferred_element_type=F32)


def _dot_nt(a, b):
    return lax.dot_general(a, b, (((1,), (1,)), ((), ())), preferred_element_type=F32)


def _dot_tn(a, b):
    return lax.dot_general(a, b, (((0,), (0,)), ((), ())), preferred_element_type=F32)


def _cumsum(x, axis, period=None):
    n = x.shape[axis]
    idx = lax.broadcasted_iota(jnp.int32, x.shape, axis)
    if period is not None:
        idx = idx % period
        n = period
    shift = 1
    while shift < n:
        x = x + jnp.where(idx >= shift, pltpu.roll(x, shift, axis), 0.0)
        shift *= 2
    return x


def _ffn_step(x_ref, wg_ref, wu_ref, wd_ref, g_ref, b_ref, out_refs, xb_ref, *, alpha, rider=None):
    f = pl.program_id(1)
    acc_ref = out_refs[0]

    @pl.when(f == 0)
    def _():
        xb_ref[...] = x_ref[...].astype(BF16)
        acc_ref[...] = jnp.zeros_like(acc_ref)

    if rider is not None:
        rider[0]()
    xb = xb_ref[...]
    hidden = _silu(_dot(xb, wg_ref[...])) * _dot(xb, wu_ref[...])
    acc_ref[...] += _dot(hidden.astype(BF16), wd_ref[...])
    if rider is not None:
        rider[1]()

    @pl.when(f == pl.num_programs(1) - 1)
    def _():
        y = _layer_norm(alpha * x_ref[...] + 0.5 * acc_ref[...], g_ref[...], b_ref[...])
        for o_ref in out_refs:
            o_ref[...] = y.astype(o_ref.dtype)

    if rider is not None:
        rider[2]()


def _ffn_kernel(x_ref, wg_ref, wu_ref, wd_ref, g_ref, b_ref, *rest, alpha):
    *out_refs, xb_ref = rest
    _ffn_step(x_ref, wg_ref, wu_ref, wd_ref, g_ref, b_ref, out_refs, xb_ref, alpha=alpha)


def _ffn_specs(T, D, F, tm, tf, emit_bf16):
    assert T % tm == 0 and F % tf == 0
    token_tile = lambda i, f, *_: (i, 0)
    in_specs = [pl.BlockSpec((tm, D), token_tile),
                pl.BlockSpec((D, tf), lambda i, f, *_: (0, f)),
                pl.BlockSpec((D, tf), lambda i, f, *_: (0, f)),
                pl.BlockSpec((tf, D), lambda i, f, *_: (f, 0)),
                pl.BlockSpec((1, D), lambda i, f, *_: (0, 0)),
                pl.BlockSpec((1, D), lambda i, f, *_: (0, 0))]
    out_dtypes = (F32, BF16) if emit_bf16 else (F32,)
    out_specs = [pl.BlockSpec((tm, D), token_tile) for _ in out_dtypes]
    out_shape = [jax.ShapeDtypeStruct((T, D), dt) for dt in out_dtypes]
    scratch = [pltpu.VMEM((tm, D), BF16)]
    block_bytes = (_nbytes((tm, D), F32) + sum(_nbytes((tm, D), dt) for dt in out_dtypes)
                   + 3 * _nbytes((D, tf), BF16))
    scratch_bytes = _nbytes((tm, D), BF16)
    return in_specs, out_specs, out_shape, scratch, block_bytes, scratch_bytes


def _ffn_sublayer(x, wg, wu, wd, g, b, *, alpha, tm, tf, emit_bf16):
    T, D = x.shape
    F = wg.shape[1]
    in_specs, out_specs, out_shape, scratch, block_bytes, scratch_bytes = _ffn_specs(T, D, F, tm, tf, emit_bf16)
    return pl.pallas_call(
        functools.partial(_ffn_kernel, alpha=alpha),
        grid=(T // tm, F // tf),
        in_specs=in_specs,
        out_specs=out_specs,
        out_shape=out_shape,
        scratch_shapes=scratch,
        compiler_params=_params(("parallel", "arbitrary"), block_bytes, scratch_bytes),
        name="ffn_sublayer",
    )(x, wg, wu, wd, g, b)


def _proj_rope_kernel(h_ref, w_ref, cos_ref, sin_ref, o_ref, *, heads, k_scale):
    scale = jnp.where(pl.program_id(0) == 0, 1.0, k_scale).astype(F32)
    z = _dot(h_ref[...], w_ref[...])
    cos = cos_ref[...]
    sin = sin_ref[...]
    hd = z.shape[1] // heads
    half = hd // 2
    for h in range(heads):
        x1 = z[:, h * hd:h * hd + half]
        x2 = z[:, h * hd + half:(h + 1) * hd]
        o_ref[:, h * hd:h * hd + half] = ((x1 * cos - x2 * sin) * scale).astype(o_ref.dtype)
        o_ref[:, h * hd + half:(h + 1) * hd] = ((x1 * sin + x2 * cos) * scale).astype(o_ref.dtype)


def _proj_rope(hb, w, cos, sin, *, tm, tn, heads, k_scale):
    T, D = hb.shape
    N = 2 * tn
    half = cos.shape[1]
    pos_blocks = cos.shape[0] // tm
    block_bytes = (_nbytes((tm, D), BF16) + _nbytes((D, tn), BF16) + 2 * _nbytes((tm, half), F32)
                   + _nbytes((tm, tn), BF16))
    return pl.pallas_call(
        functools.partial(_proj_rope_kernel, heads=heads, k_scale=k_scale),
        grid=(2, T // tm),
        in_specs=[pl.BlockSpec((tm, D), lambda j, i: (i, 0)),
                  pl.BlockSpec((D, tn), lambda j, i: (0, j)),
                  pl.BlockSpec((tm, half), lambda j, i: (i % pos_blocks, 0)),
                  pl.BlockSpec((tm, half), lambda j, i: (i % pos_blocks, 0))],
        out_specs=pl.BlockSpec((tm, tn), lambda j, i: (i, j)),
        out_shape=jax.ShapeDtypeStruct((T, N), BF16),
        compiler_params=_params(("parallel", "parallel"), block_bytes),
        name="proj_rope",
    )(hb, w, cos, sin)


def _proj_kernel(h_ref, w_ref, o_ref):
    o_ref[...] = _dot(h_ref[...], w_ref[...]).astype(o_ref.dtype)


def _proj(hb, w, *, tm, tn, col0, n_blocks):
    T, D = hb.shape
    N = n_blocks * tn
    block_bytes = _nbytes((tm, D), BF16) + _nbytes((D, tn), BF16) + _nbytes((tm, tn), BF16)
    return pl.pallas_call(
        _proj_kernel,
        grid=(n_blocks, T // tm),
        in_specs=[pl.BlockSpec((tm, D), lambda j, i: (i, 0)),
                  pl.BlockSpec((D, tn), lambda j, i: (0, col0 + j))],
        out_specs=pl.BlockSpec((tm, tn), lambda j, i: (i, j)),
        out_shape=jax.ShapeDtypeStruct((T, N), BF16),
        compiler_params=_params(("parallel", "parallel"), block_bytes),
        name="proj",
    )(hb, w)


def _proj_rows_kernel(h_ref, w_ref, rows_ref, ob_ref, *, heads):
    z = _dot(h_ref[...], w_ref[...])
    ob_ref[...] = z.astype(ob_ref.dtype)
    hd = z.shape[1] // heads
    for h in range(heads):
        rows_ref[:, h, :] = z[:, h * hd:(h + 1) * hd]


def _proj_rows(hb, w, *, tm, heads, hd, col0):
    T, D = hb.shape
    N = heads * hd
    block_bytes = (_nbytes((tm, D), BF16) + _nbytes((D, N), BF16) + _nbytes((tm, N), F32)
                   + _nbytes((tm, N), BF16))
    return pl.pallas_call(
        functools.partial(_proj_rows_kernel, heads=heads),
        grid=(T // tm,),
        in_specs=[pl.BlockSpec((tm, D), lambda i: (i, 0)),
                  pl.BlockSpec((D, N), lambda i: (0, col0))],
        out_specs=[pl.BlockSpec((tm, heads, hd), lambda i: (i, 0, 0)),
                   pl.BlockSpec((tm, N), lambda i: (i, 0))],
        out_shape=[jax.ShapeDtypeStruct((T, heads, hd), F32),
                   jax.ShapeDtypeStruct((T, N), BF16)],
        compiler_params=_params(("parallel",), block_bytes),
        name="proj_rows",
    )(hb, w)


def _fgate_prompt_kernel(h_ref, wf_ref, wft_ref, brow_ref, bcol_ref, logf_ref, c_ref, ct_ref):
    h = h_ref[0]
    logf = _log_sigmoid(_dot(h, wf_ref[...]) + brow_ref[...])
    logf_ref[0] = logf
    c_ref[0] = _cumsum(logf, 0)
    logf_t = _log_sigmoid(_dot_nt(wft_ref[...], h) + bcol_ref[...])
    ct_ref[0] = _cumsum(logf_t, 1)


def _fgate_prompt(hb3, wf, wft, brow, bcol):
    B, S, D = hb3.shape
    H = wf.shape[1]
    block_bytes = _nbytes((S, D), BF16) + 3 * _nbytes((S, LANES), F32)
    return pl.pallas_call(
        _fgate_prompt_kernel,
        grid=(B,),
        in_specs=[pl.BlockSpec((1, S, D), lambda b: (b, 0, 0)),
                  pl.BlockSpec((D, H), lambda b: (0, 0)),
                  pl.BlockSpec((H, D), lambda b: (0, 0)),
                  pl.BlockSpec((1, H), lambda b: (0, 0)),
                  pl.BlockSpec((H, 1), lambda b: (0, 0))],
        out_specs=[pl.BlockSpec((1, S, H), lambda b: (b, 0, 0)),
                   pl.BlockSpec((1, S, H), lambda b: (b, 0, 0)),
                   pl.BlockSpec((1, H, S), lambda b: (b, 0, 0))],
        out_shape=[jax.ShapeDtypeStruct((B, S, H), F32),
                   jax.ShapeDtypeStruct((B, S, H), F32),
                   jax.ShapeDtypeStruct((B, H, S), F32)],
        compiler_params=_params(("parallel",), block_bytes),
        name="fgate_prompt",
    )(hb3, wf, wft, brow, bcol)


def _fgate_sample_kernel(h_ref, wf_ref, brow_ref, logf_ref, cn_ref, *, seq):
    logf = _log_sigmoid(_dot(h_ref[...], wf_ref[...]) + brow_ref[...])
    logf_ref[...] = logf
    cn_ref[...] = _cumsum(logf, 0, period=seq)


def _fgate_sample(hb, wf, brow, *, seq):
    T, _ = hb.shape
    H = wf.shape[1]
    return pl.pallas_call(
        functools.partial(_fgate_sample_kernel, seq=seq),
        out_shape=[jax.ShapeDtypeStruct((T, H), F32), jax.ShapeDtypeStruct((T, H), F32)],
        name="fgate_sample",
    )(hb, wf, brow)


def _ret_log_gamma(h):
    return jnp.float32(np.log1p(-np.exp2(-5.0 - h)))


def _ret_decays(log_g, C):
    row = lax.broadcasted_iota(jnp.int32, (C, C), 0)
    col = lax.broadcasted_iota(jnp.int32, (C, C), 1)
    rel = (row - col).astype(F32)
    inner = jnp.where(rel >= 0, jnp.exp(log_g * jnp.maximum(rel, 0.0)), 0.0)
    idx = lax.broadcasted_iota(jnp.int32, (C, 1), 0).astype(F32)
    q_decay = jnp.exp(log_g * (idx + 1.0))
    k_decay = jnp.exp(log_g * (C - 1.0 - idx))
    chunk_decay = jnp.exp(log_g * jnp.full((1, 1), float(C), F32))
    return inner, q_decay, k_decay, chunk_decay


def _ret_chunk(q, k, v, g, state, decays):
    inner, q_decay, k_decay, chunk_decay = decays
    att = _dot_nt(q, k) * inner
    o = _dot(att.astype(BF16), v) + _dot(q, state.astype(BF16)) * q_decay
    kd = (k.astype(F32) * k_decay).astype(BF16)
    new_state = chunk_decay * state + _dot_tn(kd, v)
    rn = o * lax.rsqrt(jnp.mean(o * o, axis=-1, keepdims=True) + GN_EPS)
    return (rn * _silu(g.astype(F32))).astype(BF16), new_state


def _ret_prompt_kernel(q_ref, k_ref, v_ref, g_ref, o_ref, s_ref, *, chunk):
    rows, W = q_ref.shape
    hd = W // RET_HEADS
    decays = [_ret_decays(_ret_log_gamma(h), chunk) for h in range(RET_HEADS)]

    @pl.when(pl.program_id(1) == 0)
    def _():
        s_ref[...] = jnp.zeros_like(s_ref)

    def body(c, carry):
        sl = pl.ds(pl.multiple_of(c * chunk, chunk), chunk)
        for h in range(RET_HEADS):
            cols = slice(h * hd, (h + 1) * hd)
            out, new_state = _ret_chunk(q_ref[sl, cols], k_ref[sl, cols], v_ref[sl, cols], g_ref[sl, cols],
                                        s_ref[0, h], decays[h])
            o_ref[sl, cols] = out
            s_ref[0, h] = new_state
        return carry

    lax.fori_loop(0, rows // chunk, body, 0)


def _ret_prompt(zr, zp, *, B, S, seq_blocks):
    W = zr.shape[1] // 2
    hd = W // RET_HEADS
    chunk = RET_CHUNK if S % RET_CHUNK == 0 else S
    rows = S // seq_blocks
    assert S % seq_blocks == 0 and rows % chunk == 0
    block_bytes = 5 * _nbytes((rows, W), BF16) + _nbytes((RET_HEADS, hd, hd), F32)
    tokens = lambda col: (lambda b, i: (b * seq_blocks + i, col))
    return pl.pallas_call(
        functools.partial(_ret_prompt_kernel, chunk=chunk),
        grid=(B, seq_blocks),
        in_specs=[pl.BlockSpec((rows, W), tokens(0)),
                  pl.BlockSpec((rows, W), tokens(1)),
                  pl.BlockSpec((rows, W), tokens(0)),
                  pl.BlockSpec((rows, W), tokens(1))],
        out_specs=[pl.BlockSpec((rows, W), tokens(0)),
                   pl.BlockSpec((1, RET_HEADS, hd, hd), lambda b, i: (b, 0, 0, 0))],
        out_shape=[jax.ShapeDtypeStruct((B * S, W), BF16),
                   jax.ShapeDtypeStruct((B, RET_HEADS, hd, hd), F32)],
        compiler_params=_params(("parallel", "arbitrary"), block_bytes),
        name="retention_prompt",
    )(zr, zr, zp, zp)


def _ret_sample_kernel(zr_ref, zp_ref, s0_ref, o_ref, s_ref, *, width):
    L = zr_ref.shape[1]
    hd = width // RET_HEADS
    for h in range(RET_HEADS):
        decays = _ret_decays(_ret_log_gamma(h), L)
        q = zr_ref[0, :, h * hd:(h + 1) * hd]
        k = zr_ref[0, :, width + h * hd:width + (h + 1) * hd]
        v = zp_ref[0, :, h * hd:(h + 1) * hd]
        g = zp_ref[0, :, width + h * hd:width + (h + 1) * hd]
        out, new_state = _ret_chunk(q, k, v, g, s0_ref[0, 0, h], decays)
        o_ref[0, :, h * hd:(h + 1) * hd] = out
        s_ref[0, h] = new_state


def _ret_sample(zr3, zp3, state, layer):
    DB, L, W2 = zr3.shape
    W = W2 // 2
    hd = W // RET_HEADS
    NP = zp3.shape[2]
    block_bytes = _nbytes((16, W2 + NP + W), BF16) + 2 * _nbytes((RET_HEADS, hd, hd), F32)
    return pl.pallas_call(
        functools.partial(_ret_sample_kernel, width=W),
        grid=(DB,),
        in_specs=[pl.BlockSpec((1, L, W2), lambda b: (b, 0, 0)),
                  pl.BlockSpec((1, L, NP), lambda b: (b, 0, 0)),
                  pl.BlockSpec((1, 1, RET_HEADS, hd, hd), lambda b: (layer, b, 0, 0, 0))],
        out_specs=[pl.BlockSpec((1, L, W), lambda b: (b, 0, 0)),
                   pl.BlockSpec((1, RET_HEADS, hd, hd), lambda b: (b, 0, 0, 0))],
        out_shape=[jax.ShapeDtypeStruct((DB, L, W), BF16),
                   jax.ShapeDtypeStruct((DB, RET_HEADS, hd, hd), F32)],
        compiler_params=_params(("parallel",), block_bytes),
        name="retention_sample",
    )(zr3, zp3, state)


def _fox_prompt_kernel(q_ref, k_ref, v_ref, c_ref, ct_ref, o_ref, qt_ref, vt_ref, acc_ref, m_ref, l_ref,
                       u_ref, p_ref, *, heads, scale):
    tq = q_ref.shape[1]
    S = k_ref.shape[1]
    hd = q_ref.shape[2] // heads
    qi = pl.program_id(1)
    a_scale = scale * LOG2E

    def transposed(x):
        return x.astype(F32).T.astype(BF16)

    @pl.when(qi == 0)
    def _():
        def body(j, carry):
            ks = pl.ds(pl.multiple_of(j * tq, tq), tq)
            for h in range(heads):
                vt_ref[h * hd:(h + 1) * hd, ks] = transposed(v_ref[0, ks, h * hd:(h + 1) * hd])
            return carry
        lax.fori_loop(0, S // tq, body, 0)

    for h in range(heads):
        qt_ref[h * hd:(h + 1) * hd, :] = transposed(q_ref[0, :, h * hd:(h + 1) * hd])
    qs = pl.ds(pl.multiple_of(qi * tq, tq), tq)
    c_q = ct_ref[0, :, qs] * LOG2E
    key = lax.broadcasted_iota(jnp.int32, (tq, tq), 0)
    qry = lax.broadcasted_iota(jnp.int32, (tq, tq), 1)
    causal = key <= qry

    def block(j, first):
        ks = pl.ds(pl.multiple_of(j * tq, tq), tq)
        c_k = c_ref[0, ks, :] * LOG2E
        maxes = []
        for h in range(heads):
            rows = slice(h * hd, (h + 1) * hd)
            u = _dot(k_ref[0, ks, rows], qt_ref[rows, :]) * a_scale - c_k[:, h:h + 1]
            if first:
                u = jnp.where(causal, u, -jnp.inf)
            u_ref[h] = u
            maxes.append(jnp.max(u, axis=0, keepdims=True))
        rescale = []
        for h in range(heads):
            cq = c_q[h:h + 1, :]
            mx = maxes[h] + cq
            if first:
                m_new = mx
            else:
                m_old = m_ref[h:h + 1, :]
                m_new = jnp.maximum(m_old, mx)
            p = jnp.exp2(u_ref[h] + (cq - m_new))
            ps = jnp.sum(p, axis=0, keepdims=True)
            p_ref[h] = p.astype(BF16)
            if first:
                l_ref[h:h + 1, :] = ps
                rescale.append(None)
            else:
                a = jnp.exp2(m_old - m_new)
                l_ref[h:h + 1, :] = a * l_ref[h:h + 1, :] + ps
                rescale.append(a)
            m_ref[h:h + 1, :] = m_new
        for h in range(heads):
            rows = slice(h * hd, (h + 1) * hd)
            pv = _dot(vt_ref[rows, ks], p_ref[h])
            if first:
                acc_ref[rows, :] = pv
            else:
                acc_ref[rows, :] = rescale[h] * acc_ref[rows, :] + pv

    block(qi, True)

    def off_diagonal(j, carry):
        block(j, False)
        return carry

    lax.fori_loop(0, qi, off_diagonal, 0)
    for h in range(heads):
        rows = slice(h * hd, (h + 1) * hd)
        o_ref[0, :, rows] = (acc_ref[rows, :] / l_ref[h:h + 1, :]).T.astype(o_ref.dtype)


def _fox_prompt(fq, fkb, fvb, c, ct, *, tq, q_col):
    B, S, H = c.shape
    W = fkb.shape[2]
    scale = float((W // H) ** -0.5)
    block_bytes = (2 * _nbytes((tq, W), BF16) + 2 * _nbytes((S, W), BF16)
                   + _nbytes((S, LANES), F32) + _nbytes((H, S), F32))
    scratch_bytes = (_nbytes((W, tq), BF16) + _nbytes((W, S), BF16) + _nbytes((W, tq), F32)
                     + _nbytes((H, tq, tq), F32) + _nbytes((H, tq, tq), BF16))
    return pl.pallas_call(
        functools.partial(_fox_prompt_kernel, heads=H, scale=scale),
        grid=(B, S // tq),
        in_specs=[pl.BlockSpec((1, tq, W), lambda b, i: (b, i, q_col)),
                  pl.BlockSpec((1, S, W), lambda b, i: (b, 0, 0)),
                  pl.BlockSpec((1, S, W), lambda b, i: (b, 0, 0)),
                  pl.BlockSpec((1, S, H), lambda b, i: (b, 0, 0)),
                  pl.BlockSpec((1, H, S), lambda b, i: (b, 0, 0))],
        out_specs=pl.BlockSpec((1, tq, W), lambda b, i: (b, i, 0)),
        out_shape=jax.ShapeDtypeStruct((B, S, W), BF16),
        scratch_shapes=[pltpu.VMEM((W, tq), BF16),
                        pltpu.VMEM((W, S), BF16),
                        pltpu.VMEM((W, tq), F32),
                        pltpu.VMEM((H, tq), F32),
                        pltpu.VMEM((H, tq), F32),
                        pltpu.VMEM((H, tq, tq), F32),
                        pltpu.VMEM((H, tq, tq), BF16)],
        compiler_params=_params(("parallel", "arbitrary"), block_bytes, scratch_bytes),
        name="fox_prompt",
    )(fq, fkb, fvb, c, ct)


def _suffix_kernel(x_ref, m_ref, o_ref):
    x = x_ref[...]
    hi = x.astype(BF16)
    r1 = x - hi.astype(F32)
    mid = r1.astype(BF16)
    lo = (r1 - mid.astype(F32)).astype(BF16)
    m = m_ref[...]
    o_ref[...] = _dot(hi, m) + _dot(mid, m) + _dot(lo, m)


def _suffix_selector(rows, heads):
    src = np.arange(rows * heads)
    dst = np.arange(rows * heads)
    same_head = (src % heads)[:, None] == (dst % heads)[None, :]
    later = (src // heads)[:, None] > (dst // heads)[None, :]
    total = (src % heads)[:, None] == (np.arange(LANES) % heads)[None, :]
    return jnp.asarray(np.concatenate([same_head & later, total], axis=1), dtype=BF16)


def _page_suffix_sums(logf_flat, sel, *, tp):
    P, K = logf_flat.shape
    N = sel.shape[1]
    block_bytes = _nbytes((tp, K), F32) + _nbytes((tp, N), F32) + _nbytes((K, N), BF16)
    return pl.pallas_call(
        _suffix_kernel,
        grid=(P // tp,),
        in_specs=[pl.BlockSpec((tp, K), lambda i: (i, 0)),
                  pl.BlockSpec((K, N), lambda i: (0, 0))],
        out_specs=pl.BlockSpec((tp, N), lambda i: (i, 0)),
        out_shape=jax.ShapeDtypeStruct((P, N), F32),
        compiler_params=_params(("parallel",), block_bytes),
        name="page_suffix_sums",
    )(logf_flat, sel)


def _decode_rider(g, last_g, active, q_ref, kn_ref, vn_ref, cn_ref, cnrow_ref, k_refs, v_refs, w_refs, o_ref,
                  cq_ref, m_ref, l_ref, acc_ref, carry_ref, *, heads, seq, scale):
    pages = len(k_refs)
    rows = seq * heads
    hd = q_ref.shape[2]
    n_keys = k_refs[0].shape[2] * heads
    chunks = n_keys // LANES
    a_scale = scale * LOG2E

    def before():
        @pl.when(jnp.logical_and(g == 0, active))
        def _():
            q = q_ref[0]
            cn = cn_ref[0]
            sub = lax.broadcasted_iota(jnp.int32, (heads, heads), 0)
            ln = lax.broadcasted_iota(jnp.int32, (heads, heads), 1)
            c_q = jnp.concatenate(
                [jnp.sum(jnp.where(sub == ln, jnp.broadcast_to(cn[t:t + 1, :], (heads, heads)), 0.0),
                         axis=1, keepdims=True) for t in range(seq)], axis=0)
            c_q = c_q * LOG2E
            cq_ref[...] = jnp.broadcast_to(c_q, cq_ref.shape)
            s = _dot_nt(q, kn_ref[0]) * a_scale - cnrow_ref[0] * LOG2E
            r_i = lax.broadcasted_iota(jnp.int32, s.shape, 0)
            k_i = lax.broadcasted_iota(jnp.int32, s.shape, 1)
            keep = (r_i % heads == k_i % heads) & (k_i // heads <= r_i // heads)
            s = jnp.where(keep, s, -jnp.inf)
            m0 = jnp.max(s, axis=-1, keepdims=True) + c_q
            p = jnp.exp2(s + (c_q - m0))
            m_ref[...] = jnp.broadcast_to(m0, m_ref.shape)
            l_ref[...] = jnp.broadcast_to(jnp.sum(p, axis=-1, keepdims=True), l_ref.shape)
            acc_ref[...] = _dot(p.astype(BF16), vn_ref[0])
            carry_ref[...] = jnp.zeros_like(carry_ref)

    def main():
        pl.when(active)(update)

    def update():
        q = q_ref[0]
        same_head = (lax.broadcasted_iota(jnp.int32, (rows, n_keys), 0) % heads
                     == lax.broadcasted_iota(jnp.int32, (rows, n_keys), 1) % heads)
        c_q = cq_ref[...][:, 0:1]
        m = m_ref[...][:, 0:1]
        carry = carry_ref[...]
        s_pages = [None] * pages
        for i in reversed(range(pages)):
            w = w_refs[i][0, 0]
            bias = jnp.concatenate([w[c:c + 1, :] + carry for c in range(chunks)], axis=1) * LOG2E
            carry = carry + w[chunks:chunks + 1, :]
            k2 = k_refs[i][0, 0].reshape(n_keys, hd).astype(BF16)
            s_pages[i] = jnp.where(same_head, _dot_nt(q, k2) * a_scale + bias, -jnp.inf)
        s_max = s_pages[0]
        for i in range(1, pages):
            s_max = jnp.maximum(s_max, s_pages[i])
        m_new = jnp.maximum(m, jnp.max(s_max, axis=-1, keepdims=True) + c_q)
        a = jnp.exp2(m - m_new)
        shift = c_q - m_new
        p_sum = jnp.zeros((rows, n_keys), F32)
        acc = a * acc_ref[...]
        for i in range(pages):
            p = jnp.exp2(s_pages[i] + shift)
            p_sum = p_sum + p
            acc = acc + _dot(p.astype(BF16), v_refs[i][0, 0].reshape(n_keys, hd).astype(BF16))
        m_ref[...] = jnp.broadcast_to(m_new, m_ref.shape)
        l_ref[...] = a * l_ref[...] + jnp.sum(p_sum, axis=-1, keepdims=True)
        acc_ref[...] = acc
        carry_ref[...] = carry

    def after():
        @pl.when(jnp.logical_and(g == last_g, active))
        def _():
            o_ref[0] = (acc_ref[...] / l_ref[...][:, 0:1]).astype(o_ref.dtype)

    return before, main, after


def _ffn_decode_kernel(pt_ref, x_ref, wg_ref, wu_ref, wd_ref, g_ref, b_ref,
                       q_ref, kn_ref, vn_ref, cn_ref, cnrow_ref, *rest,
                       alpha, n_ffn_out, pages, groups, decode_steps, heads, seq, scale):
    del pt_ref
    k_refs = rest[:pages]
    v_refs = rest[pages:2 * pages]
    w_refs = rest[2 * pages:3 * pages]
    rest = rest[3 * pages:]
    out_refs = rest[:n_ffn_out]
    od_ref, xb_ref, cq_ref, m_ref, l_ref, dacc_ref, carry_ref = rest[n_ffn_out:]
    step = pl.program_id(0) * pl.num_programs(1) + pl.program_id(1)
    rider = _decode_rider(lax.rem(step, groups), groups - 1, step < decode_steps,
                          q_ref, kn_ref, vn_ref, cn_ref, cnrow_ref, k_refs, v_refs, w_refs, od_ref,
                          cq_ref, m_ref, l_ref, dacc_ref, carry_ref, heads=heads, seq=seq, scale=scale)
    _ffn_step(x_ref, wg_ref, wu_ref, wd_ref, g_ref, b_ref, out_refs, xb_ref, alpha=alpha, rider=rider)


def _ffn_sublayer_with_decode(x, wg, wu, wd, g, b, page_table, q2, kn2, vn2, cn3, cn_row, cache_k, cache_v,
                              w_sfx, layer, *, alpha, tm, tf, emit_bf16, pages, batch0, n_batches):
    T, D = x.shape
    F = wg.shape[1]
    _, rows, hd = q2.shape
    _, L, H = cn3.shape
    n_pages = page_table.shape[1]
    page = cache_k.shape[2]
    wrows = w_sfx.shape[2]
    assert n_pages % pages == 0 and page * H == (wrows - 1) * LANES and LANES % H == 0
    groups = n_pages // pages
    n_f = F // tf
    decode_steps = n_batches * groups
    assert decode_steps <= (T // tm) * n_f
    ffn_in, ffn_out, ffn_shape, ffn_scratch, block_bytes, scratch_bytes = _ffn_specs(T, D, F, tm, tf, emit_bf16)

    def decode_pos(i, f):
        d = jnp.minimum(i * n_f + f, decode_steps - 1)
        return d // groups, d % groups

    def page_map(p, trailing):
        def index(i, f, pt):
            bd, gd = decode_pos(i, f)
            return (layer, pt[batch0 + bd, (groups - 1 - gd) * pages + p]) + (0,) * trailing
        return index

    per_seq = lambda i, f, pt: (batch0 + decode_pos(i, f)[0], 0, 0)
    kv_spec = [pl.BlockSpec((1, 1, page, H, hd), page_map(p, 3)) for p in range(pages)]
    w_spec = [pl.BlockSpec((1, 1, wrows, LANES), page_map(p, 2)) for p in range(pages)]
    grid_spec = pltpu.PrefetchScalarGridSpec(
        num_scalar_prefetch=1,
        grid=(T // tm, n_f),
        in_specs=ffn_in + [pl.BlockSpec((1, rows, hd), per_seq),
                           pl.BlockSpec((1, rows, hd), per_seq),
                           pl.BlockSpec((1, rows, hd), per_seq),
                           pl.BlockSpec((1, L, H), per_seq),
                           pl.BlockSpec((1, 1, rows), per_seq)] + kv_spec + kv_spec + w_spec,
        out_specs=ffn_out + [pl.BlockSpec((1, rows, hd), lambda i, f, pt: (decode_pos(i, f)[0], 0, 0))],
        scratch_shapes=ffn_scratch + [pltpu.VMEM((rows, LANES), F32),
                                      pltpu.VMEM((rows, LANES), F32),
                                      pltpu.VMEM((rows, LANES), F32),
                                      pltpu.VMEM((rows, hd), F32),
                                      pltpu.VMEM((1, LANES), F32)])
    block_bytes += 2 * pages * _nbytes((page, H, hd), F32) + pages * _nbytes((16, LANES), F32)
    outs = pl.pallas_call(
        functools.partial(_ffn_decode_kernel, alpha=alpha, n_ffn_out=len(ffn_out), pages=pages, groups=groups,
                          decode_steps=decode_steps, heads=H, seq=L, scale=float(hd ** -0.5)),
        grid_spec=grid_spec,
        out_shape=ffn_shape + [jax.ShapeDtypeStruct((n_batches, rows, hd), BF16)],
        compiler_params=_params(("arbitrary", "arbitrary"), block_bytes, scratch_bytes),
        name="ffn_sublayer_with_decode",
    )(page_table, x, wg, wu, wd, g, b, q2, kn2, vn2, cn3, cn_row,
      *([cache_k] * pages), *([cache_v] * pages), *([w_sfx] * pages))
    return outs[:-1], outs[-1]


def _mix_out_kernel(ret_ref, fox_ref, wr_ref, wf_ref, x_ref, g_ref, b_ref, o_ref, *, alpha):
    y = alpha * x_ref[...] + _dot(ret_ref[...], wr_ref[...]) + _dot(fox_ref[...], wf_ref[...])
    o_ref[...] = _layer_norm(y, g_ref[...], b_ref[...])


def _mix_out(ret, fox, w_out, x, g, b, *, alpha, tm):
    T, D = x.shape
    Wr = ret.shape[1]
    Wf = fox.shape[1]
    assert Wr == Wf and w_out.shape[0] == Wr + Wf
    block_bytes = (_nbytes((tm, Wr + Wf), BF16) + _nbytes((Wr + Wf, D), BF16) + 2 * _nbytes((tm, D), F32))
    return pl.pallas_call(
        functools.partial(_mix_out_kernel, alpha=alpha),
        grid=(T // tm,),
        in_specs=[pl.BlockSpec((tm, Wr), lambda i: (i, 0)),
                  pl.BlockSpec((tm, Wf), lambda i: (i, 0)),
                  pl.BlockSpec((Wr, D), lambda i: (0, 0)),
                  pl.BlockSpec((Wf, D), lambda i: (1, 0)),
                  pl.BlockSpec((tm, D), lambda i: (i, 0)),
                  pl.BlockSpec((1, D), lambda i: (0, 0)),
                  pl.BlockSpec((1, D), lambda i: (0, 0))],
        out_specs=pl.BlockSpec((tm, D), lambda i: (i, 0)),
        out_shape=jax.ShapeDtypeStruct((T, D), F32),
        compiler_params=_params(("parallel",), block_bytes),
        name="mix_out",
    )(ret, fox, w_out, w_out, x, g, b)


def _rope_tables(pos, half):
    inv = ROPE_BASE ** (-jnp.arange(half, dtype=F32) / half)
    ang = pos.astype(F32)[:, None] * inv[None, :]
    return jnp.cos(ang), jnp.sin(ang)


def kernel(x_prompt, x_sample, cache_k, cache_v, cache_logf, state_ret, page_table, ln1_g, ln1_b, w_ffn1_gate, w_ffn1_up, w_ffn1_down, ln2_g, ln2_b, w_in, b_f, w_out, ln3_g, ln3_b, w_ffn2_gate, w_ffn2_up, w_ffn2_down):
    B, S, D = x_prompt.shape
    DB, L, _ = x_sample.shape
    depth, n_pool, page, H, hd_fox = cache_k.shape
    assert H == FOX_HEADS
    n_pages = page_table.shape[1]
    past_len = n_pages * page
    ret_w = RET_HEADS * state_ret.shape[-1]
    fox_w = H * hd_fox
    ret_hd = ret_w // RET_HEADS
    assert ret_w == fox_w
    cw = ret_w
    alpha = float((2 * depth) ** 0.25)
    Tp, Ts = B * S, DB * L

    tm_p = 512
    tf = 512
    tm_proj = 1024
    tq = 256
    pages_per_step = 8

    cos_p, sin_p = _rope_tables(jnp.arange(S), ret_hd // 2)
    cos_s, sin_s = _rope_tables(jnp.tile(past_len + jnp.arange(L), DB), ret_hd // 2)

    xp = x_prompt.reshape(Tp, D)
    xs = x_sample.reshape(Ts, D)
    w_sfx = _page_suffix_sums(cache_logf.reshape(depth * n_pool, page * H), _suffix_selector(page, H), tp=512)
    w_sfx = w_sfx.reshape(depth, n_pool, -1, LANES)

    outs = {k: [] for k in ("rs_p", "k_p", "v_p", "lf_p", "rs_s", "k_s", "v_s", "lf_s")}
    row = lambda v: v.reshape(1, -1)
    for l in range(depth):
        wg1, wu1, wd1 = (w.astype(BF16) for w in (w_ffn1_gate[l], w_ffn1_up[l], w_ffn1_down[l]))
        wg2, wu2, wd2 = (w.astype(BF16) for w in (w_ffn2_gate[l], w_ffn2_up[l], w_ffn2_down[l]))
        win = w_in[l].astype(BF16)
        w_f = win[:, 7 * cw:]
        bf_row = b_f[l].astype(F32).reshape(1, H)
        bf_col = b_f[l].astype(F32).reshape(H, 1)
        wo = w_out[l].astype(BF16)
        q_lo = 2 * cw

        ln1, ln2, ln3 = ((row(g_[l]), row(b_[l])) for g_, b_ in ((ln1_g, ln1_b), (ln2_g, ln2_b), (ln3_g, ln3_b)))
        k_scale = float(ret_hd ** -0.5)

        xs, xsb = _ffn_sublayer(xs, wg1, wu1, wd1, *ln1, alpha=alpha, tm=Ts, tf=tf, emit_bf16=True)
        zr_s = _proj_rope(xsb, win, cos_s, sin_s, tm=Ts, tn=cw, heads=RET_HEADS, k_scale=k_scale)
        zp_s = _proj(xsb, win, tm=Ts, tn=cw, col0=2, n_blocks=3)
        fk_s, fkb_s = _proj_rows(xsb, win, tm=Ts, heads=H, hd=hd_fox, col0=5)
        fv_s, fvb_s = _proj_rows(xsb, win, tm=Ts, heads=H, hd=hd_fox, col0=6)
        logf_s, cn = _fgate_sample(xsb, w_f, bf_row, seq=L)
        ret_s, s_new = _ret_sample(zr_s.reshape(DB, L, -1), zp_s.reshape(DB, L, -1), state_ret, l)
        decode_args = (page_table, zp_s[:, q_lo:q_lo + fox_w].reshape(DB, L * H, hd_fox),
                       fkb_s.reshape(DB, L * H, hd_fox), fvb_s.reshape(DB, L * H, hd_fox),
                       cn.reshape(DB, L, H), cn.reshape(DB, 1, L * H), cache_k, cache_v, w_sfx, l)
        half = DB // 2

        (xp, xpb), fox_s0 = _ffn_sublayer_with_decode(
            xp, wg1, wu1, wd1, *ln1, *decode_args, alpha=alpha, tm=tm_p, tf=tf, emit_bf16=True,
            pages=pages_per_step, batch0=0, n_batches=half)

        zr = _proj_rope(xpb, win, cos_p, sin_p, tm=tm_proj, tn=cw, heads=RET_HEADS, k_scale=k_scale)
        zp = _proj(xpb, win, tm=tm_proj, tn=cw, col0=2, n_blocks=3)
        fk, fkb = _proj_rows(xpb, win, tm=tm_proj, heads=H, hd=hd_fox, col0=5)
        fv, fvb = _proj_rows(xpb, win, tm=tm_proj, heads=H, hd=hd_fox, col0=6)
        logf, c, ct = _fgate_prompt(xpb.reshape(B, S, D), w_f, w_f.T, bf_row, bf_col)
        ret_o, s_fin = _ret_prompt(zr, zp, B=B, S=S, seq_blocks=2)
        fox_o = _fox_prompt(zp.reshape(B, S, -1), fkb.reshape(B, S, fox_w), fvb.reshape(B, S, fox_w), c, ct,
                            tq=tq, q_col=q_lo // fox_w)
        xp = _mix_out(ret_o, fox_o.reshape(Tp, fox_w), wo, xp, *ln2, alpha=alpha, tm=tm_p)
        outs["rs_p"].append(s_fin)
        outs["k_p"].append(fk.reshape(B, S, H, hd_fox))
        outs["v_p"].append(fv.reshape(B, S, H, hd_fox))
        outs["lf_p"].append(logf)

        (xp,), fox_s1 = _ffn_sublayer_with_decode(
            xp, wg2, wu2, wd2, *ln3, *decode_args, alpha=alpha, tm=tm_p, tf=tf, emit_bf16=False,
            pages=pages_per_step, batch0=half, n_batches=DB - half)

        fox_s = jnp.concatenate([fox_s0, fox_s1], axis=0)
        xs = _mix_out(ret_s.reshape(Ts, ret_w), fox_s.reshape(Ts, fox_w), wo, xs, *ln2, alpha=alpha, tm=Ts)
        (xs,) = _ffn_sublayer(xs, wg2, wu2, wd2, *ln3, alpha=alpha, tm=Ts, tf=tf, emit_bf16=False)
        outs["rs_s"].append(s_new)
        outs["k_s"].append(fk_s.reshape(DB, L, H, hd_fox))
        outs["v_s"].append(fv_s.reshape(DB, L, H, hd_fox))
        outs["lf_s"].append(logf_s.reshape(DB, L, H))

    stack = lambda k: outs[k][0][None] if depth == 1 else jnp.stack(outs[k])
    return (xp.reshape(B, S, D), xs.reshape(DB, L, D),
            stack("rs_p"), stack("k_p"), stack("v_p"), stack("lf_p"),
            stack("rs_s"), stack("k_s"), stack("v_s"), stack("lf_s"))
```

```python
import functools

import numpy as np
import jax
import jax.numpy as jnp
from jax import lax
from jax.experimental import pallas as pl
from jax.experimental.pallas import tpu as pltpu

F32 = jnp.float32
BF16 = jnp.bfloat16

RET_HEADS = 4
FOX_HEADS = 8
RET_CHUNK = 128
ROPE_BASE = 10000.0
LN_EPS = 1e-5
GN_EPS = 1e-6
LOG2E = float(np.log2(np.e))

LANES = 128
V7X_VMEM_BYTES = 64 * 1024 * 1024
MAX_SCOPED_VMEM_BYTES = 58 * 1024 * 1024
COMPILER_SCRATCH_BYTES = 12 * 1024 * 1024


def _nbytes(shape, dtype):
    return int(np.prod(shape)) * jnp.dtype(dtype).itemsize


def _params(semantics, block_bytes, scratch_bytes=0):
    need = 2 * block_bytes + scratch_bytes + COMPILER_SCRATCH_BYTES
    limit = int(min(need, MAX_SCOPED_VMEM_BYTES))
    return pltpu.CompilerParams(dimension_semantics=semantics, vmem_limit_bytes=limit)


def _layer_norm(y, g, b):
    mu = jnp.mean(y, axis=-1, keepdims=True)
    yc = y - mu
    var = jnp.mean(yc * yc, axis=-1, keepdims=True)
    return yc * lax.rsqrt(var + LN_EPS) * g + b


def _silu(x):
    return x * jax.nn.sigmoid(x)


def _log_sigmoid(x):
    return jnp.minimum(x, 0.0) - jnp.log1p(jnp.exp(-jnp.abs(x)))


def _dot(a, b):
    return jnp.dot(a, b, preferred_element_type=F32)


def _dot_nt(a, b):
    return lax.dot_general(a, b, (((1,), (1,)), ((), ())), preferred_element_type=F32)


def _dot_tn(a, b):
    return lax.dot_general(a, b, (((0,), (0,)), ((), ())), preferred_element_type=F32)


def _cumsum(x, axis, period=None):
    n = x.shape[axis]
    idx = lax.broadcasted_iota(jnp.int32, x.shape, axis)
    if period is not None:
        idx = idx % period
        n = period
    shift = 1
    while shift < n:
        x = x + jnp.where(idx >= shift, pltpu.roll(x, shift, axis), 0.0)
        shift *= 2
    return x


def _ffn_step(x_ref, wg_ref, wu_ref, wd_ref, g_ref, b_ref, out_refs, xb_ref, *, alpha, rider=None):
    f = pl.program_id(1)
    acc_ref = out_refs[0]

    @pl.when(f == 0)
    def _():
        xb_ref[...] = x_ref[...].astype(BF16)
        acc_ref[...] = jnp.zeros_like(acc_ref)

    if rider is not None:
        rider[0]()
    xb = xb_ref[...]
    hidden = _silu(_dot(xb, wg_ref[...])) * _dot(xb, wu_ref[...])
    acc_ref[...] += _dot(hidden.astype(BF16), wd_ref[...])
    if rider is not None:
        rider[1]()

    @pl.when(f == pl.num_programs(1) - 1)
    def _():
        y = _layer_norm(alpha * x_ref[...] + 0.5 * acc_ref[...], g_ref[...], b_ref[...])
        for o_ref in out_refs:
            o_ref[...] = y.astype(o_ref.dtype)

    if rider is not None:
        rider[2]()


def _ffn_kernel(x_ref, wg_ref, wu_ref, wd_ref, g_ref, b_ref, *rest, alpha):
    *out_refs, xb_ref = rest
    _ffn_step(x_ref, wg_ref, wu_ref, wd_ref, g_ref, b_ref, out_refs, xb_ref, alpha=alpha)


def _ffn_specs(T, D, F, tm, tf, emit_bf16):
    assert T % tm == 0 and F % tf == 0
    token_tile = lambda i, f, *_: (i, 0)
    in_specs = [pl.BlockSpec((tm, D), token_tile),
                pl.BlockSpec((D, tf), lambda i, f, *_: (0, f)),
                pl.BlockSpec((D, tf), lambda i, f, *_: (0, f)),
                pl.BlockSpec((tf, D), lambda i, f, *_: (f, 0)),
                pl.BlockSpec((1, D), lambda i, f, *_: (0, 0)),
                pl.BlockSpec((1, D), lambda i, f, *_: (0, 0))]
    out_dtypes = (F32, BF16) if emit_bf16 else (F32,)
    out_specs = [pl.BlockSpec((tm, D), token_tile) for _ in out_dtypes]
    out_shape = [jax.ShapeDtypeStruct((T, D), dt) for dt in out_dtypes]
    scratch = [pltpu.VMEM((tm, D), BF16)]
    block_bytes = (_nbytes((tm, D), F32) + sum(_nbytes((tm, D), dt) for dt in out_dtypes)
                   + 3 * _nbytes((D, tf), BF16))
    scratch_bytes = _nbytes((tm, D), BF16)
    return in_specs, out_specs, out_shape, scratch, block_bytes, scratch_bytes


def _ffn_sublayer(x, wg, wu, wd, g, b, *, alpha, tm, tf, emit_bf16):
    T, D = x.shape
    F = wg.shape[1]
    in_specs, out_specs, out_shape, scratch, block_bytes, scratch_bytes = _ffn_specs(T, D, F, tm, tf, emit_bf16)
    return pl.pallas_call(
        functools.partial(_ffn_kernel, alpha=alpha),
        grid=(T // tm, F // tf),
        in_specs=in_specs,
        out_specs=out_specs,
        out_shape=out_shape,
        scratch_shapes=scratch,
        compiler_params=_params(("parallel", "arbitrary"), block_bytes, scratch_bytes),
        name="ffn_sublayer",
    )(x, wg, wu, wd, g, b)


def _proj_rope_kernel(h_ref, w_ref, cos_ref, sin_ref, o_ref, *, heads, k_scale):
    scale = jnp.where(pl.program_id(0) == 0, 1.0, k_scale).astype(F32)
    z = _dot(h_ref[...], w_ref[...])
    cos = cos_ref[...]
    sin = sin_ref[...]
    hd = z.shape[1] // heads
    half = hd // 2
    for h in range(heads):
        x1 = z[:, h * hd:h * hd + half]
        x2 = z[:, h * hd + half:(h + 1) * hd]
        o_ref[:, h * hd:h * hd + half] = ((x1 * cos - x2 * sin) * scale).astype(o_ref.dtype)
        o_ref[:, h * hd + half:(h + 1) * hd] = ((x1 * sin + x2 * cos) * scale).astype(o_ref.dtype)


def _proj_rope(hb, w, cos, sin, *, tm, tn, heads, k_scale):
    T, D = hb.shape
    N = 2 * tn
    half = cos.shape[1]
    pos_blocks = cos.shape[0] // tm
    block_bytes = (_nbytes((tm, D), BF16) + _nbytes((D, tn), BF16) + 2 * _nbytes((tm, half), F32)
                   + _nbytes((tm, tn), BF16))
    return pl.pallas_call(
        functools.partial(_proj_rope_kernel, heads=heads, k_scale=k_scale),
        grid=(2, T // tm),
        in_specs=[pl.BlockSpec((tm, D), lambda j, i: (i, 0)),
                  pl.BlockSpec((D, tn), lambda j, i: (0, j)),
                  pl.BlockSpec((tm, half), lambda j, i: (i % pos_blocks, 0)),
                  pl.BlockSpec((tm, half), lambda j, i: (i % pos_blocks, 0))],
        out_specs=pl.BlockSpec((tm, tn), lambda j, i: (i, j)),
        out_shape=jax.ShapeDtypeStruct((T, N), BF16),
        compiler_params=_params(("parallel", "parallel"), block_bytes),
        name="proj_rope",
    )(hb, w, cos, sin)


def _proj_kernel(h_ref, w_ref, o_ref):
    o_ref[...] = _dot(h_ref[...], w_ref[...]).astype(o_ref.dtype)


def _proj(hb, w, *, tm, tn, col0, n_blocks):
    T, D = hb.shape
    N = n_blocks * tn
    block_bytes = _nbytes((tm, D), BF16) + _nbytes((D, tn), BF16) + _nbytes((tm, tn), BF16)
    return pl.pallas_call(
        _proj_kernel,
        grid=(n_blocks, T // tm),
        in_specs=[pl.BlockSpec((tm, D), lambda j, i: (i, 0)),
                  pl.BlockSpec((D, tn), lambda j, i: (0, col0 + j))],
        out_specs=pl.BlockSpec((tm, tn), lambda j, i: (i, j)),
        out_shape=jax.ShapeDtypeStruct((T, N), BF16),
        compiler_params=_params(("parallel", "parallel"), block_bytes),
        name="proj",
    )(hb, w)


def _proj_rows_kernel(h_ref, w_ref, rows_ref, ob_ref, *, heads):
    z = _dot(h_ref[...], w_ref[...])
    ob_ref[...] = z.astype(ob_ref.dtype)
    hd = z.shape[1] // heads
    per_head = jnp.stack([z[:, h * hd:(h + 1) * hd] for h in range(heads)], axis=0)
    rows_ref[...] = pltpu.einshape("htd->thd", per_head)


def _proj_rows(hb, w, *, tm, heads, hd, col0):
    T, D = hb.shape
    N = heads * hd
    block_bytes = (_nbytes((tm, D), BF16) + _nbytes((D, N), BF16) + _nbytes((tm, N), F32)
                   + _nbytes((tm, N), BF16))
    return pl.pallas_call(
        functools.partial(_proj_rows_kernel, heads=heads),
        grid=(T // tm,),
        in_specs=[pl.BlockSpec((tm, D), lambda i: (i, 0)),
                  pl.BlockSpec((D, N), lambda i: (0, col0))],
        out_specs=[pl.BlockSpec((tm, heads, hd), lambda i: (i, 0, 0)),
                   pl.BlockSpec((tm, N), lambda i: (i, 0))],
        out_shape=[jax.ShapeDtypeStruct((T, heads, hd), F32),
                   jax.ShapeDtypeStruct((T, N), BF16)],
        compiler_params=_params(("parallel",), block_bytes),
        name="proj_rows",
    )(hb, w)


def _fgate_prompt_kernel(h_ref, wf_ref, wft_ref, brow_ref, bcol_ref, logf_ref, c_ref, ct_ref):
    h = h_ref[0]
    logf = _log_sigmoid(_dot(h, wf_ref[...]) + brow_ref[...])
    logf_ref[0] = logf
    c_ref[0] = _cumsum(logf, 0)
    logf_t = _log_sigmoid(_dot_nt(wft_ref[...], h) + bcol_ref[...])
    ct_ref[0] = _cumsum(logf_t, 1)


def _fgate_prompt(hb3, wf, wft, brow, bcol):
    B, S, D = hb3.shape
    H = wf.shape[1]
    block_bytes = _nbytes((S, D), BF16) + 3 * _nbytes((S, LANES), F32)
    return pl.pallas_call(
        _fgate_prompt_kernel,
        grid=(B,),
        in_specs=[pl.BlockSpec((1, S, D), lambda b: (b, 0, 0)),
                  pl.BlockSpec((D, H), lambda b: (0, 0)),
                  pl.BlockSpec((H, D), lambda b: (0, 0)),
                  pl.BlockSpec((1, H), lambda b: (0, 0)),
                  pl.BlockSpec((H, 1), lambda b: (0, 0))],
        out_specs=[pl.BlockSpec((1, S, H), lambda b: (b, 0, 0)),
                   pl.BlockSpec((1, S, H), lambda b: (b, 0, 0)),
                   pl.BlockSpec((1, H, S), lambda b: (b, 0, 0))],
        out_shape=[jax.ShapeDtypeStruct((B, S, H), F32),
                   jax.ShapeDtypeStruct((B, S, H), F32),
                   jax.ShapeDtypeStruct((B, H, S), F32)],
        compiler_params=_params(("parallel",), block_bytes),
        name="fgate_prompt",
    )(hb3, wf, wft, brow, bcol)


def _fgate_sample_kernel(h_ref, wf_ref, brow_ref, logf_ref, cn_ref, *, seq):
    logf = _log_sigmoid(_dot(h_ref[...], wf_ref[...]) + brow_ref[...])
    logf_ref[...] = logf
    cn_ref[...] = _cumsum(logf, 0, period=seq)


def _fgate_sample(hb, wf, brow, *, seq):
    T, _ = hb.shape
    H = wf.shape[1]
    return pl.pallas_call(
        functools.partial(_fgate_sample_kernel, seq=seq),
        out_shape=[jax.ShapeDtypeStruct((T, H), F32), jax.ShapeDtypeStruct((T, H), F32)],
        name="fgate_sample",
    )(hb, wf, brow)


def _ret_log_gamma(h):
    return jnp.float32(np.log1p(-np.exp2(-5.0 - h)))


def _ret_decays(log_g, C):
    row = lax.broadcasted_iota(jnp.int32, (C, C), 0)
    col = lax.broadcasted_iota(jnp.int32, (C, C), 1)
    rel = (row - col).astype(F32)
    inner = jnp.where(rel >= 0, jnp.exp(log_g * jnp.maximum(rel, 0.0)), 0.0)
    idx = lax.broadcasted_iota(jnp.int32, (C, 1), 0).astype(F32)
    q_decay = jnp.exp(log_g * (idx + 1.0))
    k_decay = jnp.exp(log_g * (C - 1.0 - idx))
    chunk_decay = jnp.exp(log_g * jnp.full((1, 1), float(C), F32))
    return inner, q_decay, k_decay, chunk_decay


def _ret_chunk(heads):
    first = []
    for q, k, v, g, state, (inner, q_decay, k_decay, chunk_decay) in heads:
        att = _dot_nt(q, k) * inner
        cross = _dot(q, state.astype(BF16)) * q_decay
        kd = (k.astype(F32) * k_decay).astype(BF16)
        first.append((att, cross, chunk_decay * state + _dot_tn(kd, v)))
    outs = []
    for (q, k, v, g, state, _), (att, cross, new_state) in zip(heads, first):
        o = _dot(att.astype(BF16), v) + cross
        rn = o * lax.rsqrt(jnp.mean(o * o, axis=-1, keepdims=True) + GN_EPS)
        outs.append(((rn * _silu(g.astype(F32))).astype(BF16), new_state))
    return outs


def _ret_prompt_kernel(q_ref, k_ref, v_ref, g_ref, o_ref, s_ref, *, chunk):
    rows, W = q_ref.shape
    hd = W // RET_HEADS
    decays = [_ret_decays(_ret_log_gamma(h), chunk) for h in range(RET_HEADS)]

    @pl.when(pl.program_id(1) == 0)
    def _():
        s_ref[...] = jnp.zeros_like(s_ref)

    def body(c, carry):
        sl = pl.ds(pl.multiple_of(c * chunk, chunk), chunk)
        cols = [slice(h * hd, (h + 1) * hd) for h in range(RET_HEADS)]
        outs = _ret_chunk([(q_ref[sl, c], k_ref[sl, c], v_ref[sl, c], g_ref[sl, c], s_ref[0, h], decays[h])
                           for h, c in enumerate(cols)])
        for h, (out, new_state) in enumerate(outs):
            o_ref[sl, cols[h]] = out
            s_ref[0, h] = new_state
        return carry

    lax.fori_loop(0, rows // chunk, body, 0)


def _ret_prompt(zr, zp, *, B, S, seq_blocks):
    W = zr.shape[1] // 2
    hd = W // RET_HEADS
    chunk = RET_CHUNK if S % RET_CHUNK == 0 else S
    rows = S // seq_blocks
    assert S % seq_blocks == 0 and rows % chunk == 0
    block_bytes = 5 * _nbytes((rows, W), BF16) + _nbytes((RET_HEADS, hd, hd), F32)
    tokens = lambda col: (lambda b, i: (b * seq_blocks + i, col))
    return pl.pallas_call(
        functools.partial(_ret_prompt_kernel, chunk=chunk),
        grid=(B, seq_blocks),
        in_specs=[pl.BlockSpec((rows, W), tokens(0)),
                  pl.BlockSpec((rows, W), tokens(1)),
                  pl.BlockSpec((rows, W), tokens(0)),
                  pl.BlockSpec((rows, W), tokens(1))],
        out_specs=[pl.BlockSpec((rows, W), tokens(0)),
                   pl.BlockSpec((1, RET_HEADS, hd, hd), lambda b, i: (b, 0, 0, 0))],
        out_shape=[jax.ShapeDtypeStruct((B * S, W), BF16),
                   jax.ShapeDtypeStruct((B, RET_HEADS, hd, hd), F32)],
        compiler_params=_params(("parallel", "arbitrary"), block_bytes),
        name="retention_prompt",
    )(zr, zr, zp, zp)


def _ret_sample_kernel(zr_ref, zp_ref, s0_ref, o_ref, s_ref, *, width):
    L = zr_ref.shape[1]
    hd = width // RET_HEADS
    heads = []
    for h in range(RET_HEADS):
        lo, hi = h * hd, (h + 1) * hd
        heads.append((zr_ref[0, :, lo:hi], zr_ref[0, :, width + lo:width + hi],
                      zp_ref[0, :, lo:hi], zp_ref[0, :, width + lo:width + hi],
                      s0_ref[0, 0, h], _ret_decays(_ret_log_gamma(h), L)))
    for h, (out, new_state) in enumerate(_ret_chunk(heads)):
        o_ref[0, :, h * hd:(h + 1) * hd] = out
        s_ref[0, h] = new_state


def _ret_sample(zr3, zp3, state, layer):
    DB, L, W2 = zr3.shape
    W = W2 // 2
    hd = W // RET_HEADS
    NP = zp3.shape[2]
    block_bytes = _nbytes((16, W2 + NP + W), BF16) + 2 * _nbytes((RET_HEADS, hd, hd), F32)
    return pl.pallas_call(
        functools.partial(_ret_sample_kernel, width=W),
        grid=(DB,),
        in_specs=[pl.BlockSpec((1, L, W2), lambda b: (b, 0, 0)),
                  pl.BlockSpec((1, L, NP), lambda b: (b, 0, 0)),
                  pl.BlockSpec((1, 1, RET_HEADS, hd, hd), lambda b: (layer, b, 0, 0, 0))],
        out_specs=[pl.BlockSpec((1, L, W), lambda b: (b, 0, 0)),
                   pl.BlockSpec((1, RET_HEADS, hd, hd), lambda b: (b, 0, 0, 0))],
        out_shape=[jax.ShapeDtypeStruct((DB, L, W), BF16),
                   jax.ShapeDtypeStruct((DB, RET_HEADS, hd, hd), F32)],
        compiler_params=_params(("parallel",), block_bytes),
        name="retention_sample",
    )(zr3, zp3, state)


def _fox_prompt_kernel(q_ref, k_ref, v_ref, c_ref, ct_ref, o_ref, qt_ref, vt_ref, acc_ref, m_ref, l_ref,
                       u_ref, p_ref, *, heads, scale):
    tq = q_ref.shape[1]
    S = k_ref.shape[1]
    hd = q_ref.shape[2] // heads
    qi = pl.program_id(1)
    a_scale = scale * LOG2E

    def transposed(x):
        return x.astype(F32).T.astype(BF16)

    @pl.when(qi == 0)
    def _():
        def body(j, carry):
            ks = pl.ds(pl.multiple_of(j * tq, tq), tq)
            for h in range(heads):
                vt_ref[h * hd:(h + 1) * hd, ks] = transposed(v_ref[0, ks, h * hd:(h + 1) * hd])
            return carry
        lax.fori_loop(0, S // tq, body, 0)

    for h in range(heads):
        qt_ref[h * hd:(h + 1) * hd, :] = transposed(q_ref[0, :, h * hd:(h + 1) * hd])
    qs = pl.ds(pl.multiple_of(qi * tq, tq), tq)
    c_q = ct_ref[0, :, qs] * LOG2E
    key = lax.broadcasted_iota(jnp.int32, (tq, tq), 0)
    qry = lax.broadcasted_iota(jnp.int32, (tq, tq), 1)
    causal = key <= qry

    def block(j, first):
        ks = pl.ds(pl.multiple_of(j * tq, tq), tq)
        c_k = c_ref[0, ks, :] * LOG2E
        maxes = []
        for h in range(heads):
            rows = slice(h * hd, (h + 1) * hd)
            u = _dot(k_ref[0, ks, rows], qt_ref[rows, :]) * a_scale - c_k[:, h:h + 1]
            if first:
                u = jnp.where(causal, u, -jnp.inf)
            u_ref[h] = u
            maxes.append(jnp.max(u, axis=0, keepdims=True))
        rescale = []
        for h in range(heads):
            cq = c_q[h:h + 1, :]
            mx = maxes[h] + cq
            if first:
                m_new = mx
            else:
                m_old = m_ref[h:h + 1, :]
                m_new = jnp.maximum(m_old, mx)
            p = jnp.exp2(u_ref[h] + (cq - m_new))
            ps = jnp.sum(p, axis=0, keepdims=True)
            p_ref[h] = p.astype(BF16)
            if first:
                l_ref[h:h + 1, :] = ps
                rescale.append(None)
            else:
                a = jnp.exp2(m_old - m_new)
                l_ref[h:h + 1, :] = a * l_ref[h:h + 1, :] + ps
                rescale.append(a)
            m_ref[h:h + 1, :] = m_new
        for h in range(heads):
            rows = slice(h * hd, (h + 1) * hd)
            pv = _dot(vt_ref[rows, ks], p_ref[h])
            if first:
                acc_ref[rows, :] = pv
            else:
                acc_ref[rows, :] = rescale[h] * acc_ref[rows, :] + pv

    block(qi, True)

    def off_diagonal(j, carry):
        block(j, False)
        return carry

    lax.fori_loop(0, qi, off_diagonal, 0)
    for h in range(heads):
        rows = slice(h * hd, (h + 1) * hd)
        o_ref[0, :, rows] = (acc_ref[rows, :] / l_ref[h:h + 1, :]).T.astype(o_ref.dtype)


def _fox_prompt(fq, fkb, fvb, c, ct, *, tq, q_col):
    B, S, H = c.shape
    W = fkb.shape[2]
    scale = float((W // H) ** -0.5)
    block_bytes = (2 * _nbytes((tq, W), BF16) + 2 * _nbytes((S, W), BF16)
                   + _nbytes((S, LANES), F32) + _nbytes((H, S), F32))
    scratch_bytes = (_nbytes((W, tq), BF16) + _nbytes((W, S), BF16) + _nbytes((W, tq), F32)
                     + _nbytes((H, tq, tq), F32) + _nbytes((H, tq, tq), BF16))
    return pl.pallas_call(
        functools.partial(_fox_prompt_kernel, heads=H, scale=scale),
        grid=(B, S // tq),
        in_specs=[pl.BlockSpec((1, tq, W), lambda b, i: (b, i, q_col)),
                  pl.BlockSpec((1, S, W), lambda b, i: (b, 0, 0)),
                  pl.BlockSpec((1, S, W), lambda b, i: (b, 0, 0)),
                  pl.BlockSpec((1, S, H), lambda b, i: (b, 0, 0)),
                  pl.BlockSpec((1, H, S), lambda b, i: (b, 0, 0))],
        out_specs=pl.BlockSpec((1, tq, W), lambda b, i: (b, i, 0)),
        out_shape=jax.ShapeDtypeStruct((B, S, W), BF16),
        scratch_shapes=[pltpu.VMEM((W, tq), BF16),
                        pltpu.VMEM((W, S), BF16),
                        pltpu.VMEM((W, tq), F32),
                        pltpu.VMEM((H, tq), F32),
                        pltpu.VMEM((H, tq), F32),
                        pltpu.VMEM((H, tq, tq), F32),
                        pltpu.VMEM((H, tq, tq), BF16)],
        compiler_params=_params(("parallel", "arbitrary"), block_bytes, scratch_bytes),
        name="fox_prompt",
    )(fq, fkb, fvb, c, ct)


def _suffix_kernel(x_ref, m_ref, o_ref):
    x = x_ref[...]
    hi = x.astype(BF16)
    r1 = x - hi.astype(F32)
    mid = r1.astype(BF16)
    lo = (r1 - mid.astype(F32)).astype(BF16)
    m = m_ref[...]
    sums = _dot(hi, m) + _dot(mid, m) + _dot(lo, m)
    for c in range(o_ref.shape[1]):
        o_ref[:, c, :] = sums[:, c * LANES:(c + 1) * LANES]


def _suffix_selector(rows, heads):
    src = np.arange(rows * heads)
    dst = np.arange(rows * heads)
    same_head = (src % heads)[:, None] == (dst % heads)[None, :]
    later = (src // heads)[:, None] > (dst // heads)[None, :]
    total = (src % heads)[:, None] == (np.arange(LANES) % heads)[None, :]
    return jnp.asarray(np.concatenate([same_head & later, total], axis=1), dtype=BF16)


def _page_suffix_sums(logf_flat, sel, *, tp):
    P, K = logf_flat.shape
    N = sel.shape[1]
    block_bytes = _nbytes((tp, K), F32) + _nbytes((tp, 2 * N), F32) + _nbytes((K, N), BF16)
    return pl.pallas_call(
        _suffix_kernel,
        grid=(P // tp,),
        in_specs=[pl.BlockSpec((tp, K), lambda i: (i, 0)),
                  pl.BlockSpec((K, N), lambda i: (0, 0))],
        out_specs=pl.BlockSpec((tp, N // LANES, LANES), lambda i: (i, 0, 0)),
        out_shape=jax.ShapeDtypeStruct((P, N // LANES, LANES), F32),
        compiler_params=_params(("parallel",), block_bytes),
        name="page_suffix_sums",
    )(logf_flat, sel)


def _decode_rider(g, last_g, active, q_ref, kn_ref, vn_ref, cn_ref, cnrow_ref, k_refs, v_refs, w_refs, o_ref,
                  cq_ref, m_ref, l_ref, acc_ref, carry_ref, *, heads, seq, scale):
    pages = len(k_refs)
    rows = seq * heads
    hd = q_ref.shape[2]
    n_keys = k_refs[0].shape[2] * heads
    chunks = n_keys // LANES
    a_scale = scale * LOG2E

    def before():
        @pl.when(jnp.logical_and(g == 0, active))
        def _():
            q = q_ref[0]
            cn = cn_ref[0]
            sub = lax.broadcasted_iota(jnp.int32, (heads, heads), 0)
            ln = lax.broadcasted_iota(jnp.int32, (heads, heads), 1)
            c_q = jnp.concatenate(
                [jnp.sum(jnp.where(sub == ln, jnp.broadcast_to(cn[t:t + 1, :], (heads, heads)), 0.0),
                         axis=1, keepdims=True) for t in range(seq)], axis=0)
            c_q = c_q * LOG2E
            cq_ref[...] = jnp.broadcast_to(c_q, cq_ref.shape)
            s = _dot_nt(q, kn_ref[0]) * a_scale - cnrow_ref[0] * LOG2E
            r_i = lax.broadcasted_iota(jnp.int32, s.shape, 0)
            k_i = lax.broadcasted_iota(jnp.int32, s.shape, 1)
            keep = (r_i % heads == k_i % heads) & (k_i // heads <= r_i // heads)
            s = jnp.where(keep, s, -jnp.inf)
            m0 = jnp.max(s, axis=-1, keepdims=True) + c_q
            p = jnp.exp2(s + (c_q - m0))
            m_ref[...] = jnp.broadcast_to(m0, m_ref.shape)
            l_ref[...] = jnp.broadcast_to(jnp.sum(p, axis=-1, keepdims=True), l_ref.shape)
            acc_ref[...] = _dot(p.astype(BF16), vn_ref[0])
            carry_ref[...] = jnp.zeros_like(carry_ref)

    def main():
        pl.when(active)(update)

    def update():
        q = q_ref[0]
        same_head = (lax.broadcasted_iota(jnp.int32, (rows, n_keys), 0) % heads
                     == lax.broadcasted_iota(jnp.int32, (rows, n_keys), 1) % heads)
        c_q = cq_ref[...][:, 0:1]
        m = m_ref[...][:, 0:1]
        carry = carry_ref[...]
        s_pages = [None] * pages
        for i in reversed(range(pages)):
            w = w_refs[i][0, 0]
            bias = jnp.concatenate([w[c:c + 1, :] + carry for c in range(chunks)], axis=1) * LOG2E
            carry = carry + w[chunks:chunks + 1, :]
            k2 = k_refs[i][0, 0].reshape(n_keys, hd).astype(BF16)
            s_pages[i] = jnp.where(same_head, _dot_nt(q, k2) * a_scale + bias, -jnp.inf)
        s_max = s_pages[0]
        for i in range(1, pages):
            s_max = jnp.maximum(s_max, s_pages[i])
        m_new = jnp.maximum(m, jnp.max(s_max, axis=-1, keepdims=True) + c_q)
        a = jnp.exp2(m - m_new)
        shift = c_q - m_new
        p_sum = jnp.zeros((rows, n_keys), F32)
        acc = a * acc_ref[...]
        for i in range(pages):
            p = jnp.exp2(s_pages[i] + shift)
            p_sum = p_sum + p
            acc = acc + _dot(p.astype(BF16), v_refs[i][0, 0].reshape(n_keys, hd).astype(BF16))
        m_ref[...] = jnp.broadcast_to(m_new, m_ref.shape)
        l_ref[...] = a * l_ref[...] + jnp.sum(p_sum, axis=-1, keepdims=True)
        acc_ref[...] = acc
        carry_ref[...] = carry

    def after():
        @pl.when(jnp.logical_and(g == last_g, active))
        def _():
            o_ref[0] = (acc_ref[...] / l_ref[...][:, 0:1]).astype(o_ref.dtype)

    return before, main, after


def _ffn_decode_kernel(pt_ref, x_ref, wg_ref, wu_ref, wd_ref, g_ref, b_ref,
                       q_ref, kn_ref, vn_ref, cn_ref, cnrow_ref, *rest,
                       alpha, n_ffn_out, pages, groups, decode_steps, heads, seq, scale):
    del pt_ref
    k_refs = rest[:pages]
    v_refs = rest[pages:2 * pages]
    w_refs = rest[2 * pages:3 * pages]
    rest = rest[3 * pages:]
    out_refs = rest[:n_ffn_out]
    od_ref, xb_ref, cq_ref, m_ref, l_ref, dacc_ref, carry_ref = rest[n_ffn_out:]
    step = pl.program_id(0) * pl.num_programs(1) + pl.program_id(1)
    rider = _decode_rider(lax.rem(step, groups), groups - 1, step < decode_steps,
                          q_ref, kn_ref, vn_ref, cn_ref, cnrow_ref, k_refs, v_refs, w_refs, od_ref,
                          cq_ref, m_ref, l_ref, dacc_ref, carry_ref, heads=heads, seq=seq, scale=scale)
    _ffn_step(x_ref, wg_ref, wu_ref, wd_ref, g_ref, b_ref, out_refs, xb_ref, alpha=alpha, rider=rider)


def _ffn_sublayer_with_decode(x, wg, wu, wd, g, b, page_table, q2, kn2, vn2, cn3, cn_row, cache_k, cache_v,
                              w_sfx, layer, *, alpha, tm, tf, emit_bf16, pages, batch0, n_batches):
    T, D = x.shape
    F = wg.shape[1]
    _, rows, hd = q2.shape
    _, L, H = cn3.shape
    n_pages = page_table.shape[1]
    page = cache_k.shape[2]
    wrows = w_sfx.shape[2]
    assert n_pages % pages == 0 and page * H == (wrows - 1) * LANES and LANES % H == 0
    groups = n_pages // pages
    n_f = F // tf
    decode_steps = n_batches * groups
    assert decode_steps <= (T // tm) * n_f
    ffn_in, ffn_out, ffn_shape, ffn_scratch, block_bytes, scratch_bytes = _ffn_specs(T, D, F, tm, tf, emit_bf16)

    def decode_pos(i, f):
        d = jnp.minimum(i * n_f + f, decode_steps - 1)
        return d // groups, d % groups

    def page_map(p, trailing):
        def index(i, f, pt):
            bd, gd = decode_pos(i, f)
            return (layer, pt[batch0 + bd, (groups - 1 - gd) * pages + p]) + (0,) * trailing
        return index

    per_seq = lambda i, f, pt: (batch0 + decode_pos(i, f)[0], 0, 0)
    kv_spec = [pl.BlockSpec((1, 1, page, H, hd), page_map(p, 3)) for p in range(pages)]
    w_spec = [pl.BlockSpec((1, 1, wrows, LANES), page_map(p, 2)) for p in range(pages)]
    grid_spec = pltpu.PrefetchScalarGridSpec(
        num_scalar_prefetch=1,
        grid=(T // tm, n_f),
        in_specs=ffn_in + [pl.BlockSpec((1, rows, hd), per_seq),
                           pl.BlockSpec((1, rows, hd), per_seq),
                           pl.BlockSpec((1, rows, hd), per_seq),
                           pl.BlockSpec((1, L, H), per_seq),
                           pl.BlockSpec((1, 1, rows), per_seq)] + kv_spec + kv_spec + w_spec,
        out_specs=ffn_out + [pl.BlockSpec((1, rows, hd), lambda i, f, pt: (decode_pos(i, f)[0], 0, 0))],
        scratch_shapes=ffn_scratch + [pltpu.VMEM((rows, LANES), F32),
                                      pltpu.VMEM((rows, LANES), F32),
                                      pltpu.VMEM((rows, LANES), F32),
                                      pltpu.VMEM((rows, hd), F32),
                                      pltpu.VMEM((1, LANES), F32)])
    block_bytes += 2 * pages * _nbytes((page, H, hd), F32) + pages * _nbytes((16, LANES), F32)
    outs = pl.pallas_call(
        functools.partial(_ffn_decode_kernel, alpha=alpha, n_ffn_out=len(ffn_out), pages=pages, groups=groups,
                          decode_steps=decode_steps, heads=H, seq=L, scale=float(hd ** -0.5)),
        grid_spec=grid_spec,
        out_shape=ffn_shape + [jax.ShapeDtypeStruct((n_batches, rows, hd), BF16)],
        compiler_params=_params(("arbitrary", "arbitrary"), block_bytes, scratch_bytes),
        name="ffn_sublayer_with_decode",
    )(page_table, x, wg, wu, wd, g, b, q2, kn2, vn2, cn3, cn_row,
      *([cache_k] * pages), *([cache_v] * pages), *([w_sfx] * pages))
    return outs[:-1], outs[-1]


def _mix_out_kernel(ret_ref, fox_ref, wr_ref, wf_ref, x_ref, g_ref, b_ref, o_ref, *, alpha, row_chunks):
    rows = o_ref.shape[0] // row_chunks
    for r in range(row_chunks):
        sl = slice(r * rows, (r + 1) * rows)
        y = alpha * x_ref[sl, :] + _dot(ret_ref[sl, :], wr_ref[...]) + _dot(fox_ref[sl, :], wf_ref[...])
        o_ref[sl, :] = _layer_norm(y, g_ref[...], b_ref[...])


def _mix_out(ret, fox, w_out, x, g, b, *, alpha, tm):
    T, D = x.shape
    Wr = ret.shape[1]
    Wf = fox.shape[1]
    assert Wr == Wf and w_out.shape[0] == Wr + Wf
    block_bytes = (_nbytes((tm, Wr + Wf), BF16) + _nbytes((Wr + Wf, D), BF16) + 2 * _nbytes((tm, D), F32))
    return pl.pallas_call(
        functools.partial(_mix_out_kernel, alpha=alpha, row_chunks=max(1, tm // 128)),
        grid=(T // tm,),
        in_specs=[pl.BlockSpec((tm, Wr), lambda i: (i, 0)),
                  pl.BlockSpec((tm, Wf), lambda i: (i, 0)),
                  pl.BlockSpec((Wr, D), lambda i: (0, 0)),
                  pl.BlockSpec((Wf, D), lambda i: (1, 0)),
                  pl.BlockSpec((tm, D), lambda i: (i, 0)),
                  pl.BlockSpec((1, D), lambda i: (0, 0)),
                  pl.BlockSpec((1, D), lambda i: (0, 0))],
        out_specs=pl.BlockSpec((tm, D), lambda i: (i, 0)),
        out_shape=jax.ShapeDtypeStruct((T, D), F32),
        compiler_params=_params(("parallel",), block_bytes),
        name="mix_out",
    )(ret, fox, w_out, w_out, x, g, b)


def _rope_tables(pos, half):
    inv = ROPE_BASE ** (-jnp.arange(half, dtype=F32) / half)
    ang = pos.astype(F32)[:, None] * inv[None, :]
    return jnp.cos(ang), jnp.sin(ang)


def kernel(x_prompt, x_sample, cache_k, cache_v, cache_logf, state_ret, page_table, ln1_g, ln1_b, w_ffn1_gate, w_ffn1_up, w_ffn1_down, ln2_g, ln2_b, w_in, b_f, w_out, ln3_g, ln3_b, w_ffn2_gate, w_ffn2_up, w_ffn2_down):
    B, S, D = x_prompt.shape
    DB, L, _ = x_sample.shape
    depth, n_pool, page, H, hd_fox = cache_k.shape
    assert H == FOX_HEADS
    n_pages = page_table.shape[1]
    past_len = n_pages * page
    ret_w = RET_HEADS * state_ret.shape[-1]
    fox_w = H * hd_fox
    ret_hd = ret_w // RET_HEADS
    assert ret_w == fox_w
    cw = ret_w
    alpha = float((2 * depth) ** 0.25)
    Tp, Ts = B * S, DB * L

    tm_p = 512
    tf = 512
    tm_proj = 1024
    tq = 256
    pages_per_step = 8

    cos_p, sin_p = _rope_tables(jnp.arange(S), ret_hd // 2)
    cos_s, sin_s = _rope_tables(jnp.tile(past_len + jnp.arange(L), DB), ret_hd // 2)

    xp = x_prompt.reshape(Tp, D)
    xs = x_sample.reshape(Ts, D)
    w_sfx = _page_suffix_sums(cache_logf.reshape(depth * n_pool, page * H), _suffix_selector(page, H), tp=512)
    w_sfx = w_sfx.reshape(depth, n_pool, *w_sfx.shape[1:])

    outs = {k: [] for k in ("rs_p", "k_p", "v_p", "lf_p", "rs_s", "k_s", "v_s", "lf_s")}
    row = lambda v: v.reshape(1, -1)
    for l in range(depth):
        wg1, wu1, wd1 = (w.astype(BF16) for w in (w_ffn1_gate[l], w_ffn1_up[l], w_ffn1_down[l]))
        wg2, wu2, wd2 = (w.astype(BF16) for w in (w_ffn2_gate[l], w_ffn2_up[l], w_ffn2_down[l]))
        win = w_in[l][:, :7 * cw].astype(BF16)
        w_f = w_in[l][:, 7 * cw:].astype(BF16)
        bf_row = b_f[l].astype(F32).reshape(1, H)
        bf_col = b_f[l].astype(F32).reshape(H, 1)
        wo = w_out[l].astype(BF16)
        q_lo = 2 * cw

        ln1, ln2, ln3 = ((row(g_[l]), row(b_[l])) for g_, b_ in ((ln1_g, ln1_b), (ln2_g, ln2_b), (ln3_g, ln3_b)))
        k_scale = float(ret_hd ** -0.5)

        xs, xsb = _ffn_sublayer(xs, wg1, wu1, wd1, *ln1, alpha=alpha, tm=Ts, tf=tf, emit_bf16=True)
        zr_s = _proj_rope(xsb, win, cos_s, sin_s, tm=Ts, tn=cw, heads=RET_HEADS, k_scale=k_scale)
        zp_s = _proj(xsb, win, tm=Ts, tn=cw, col0=2, n_blocks=3)
        fk_s, fkb_s = _proj_rows(xsb, win, tm=Ts, heads=H, hd=hd_fox, col0=5)
        fv_s, fvb_s = _proj_rows(xsb, win, tm=Ts, heads=H, hd=hd_fox, col0=6)
        logf_s, cn = _fgate_sample(xsb, w_f, bf_row, seq=L)
        ret_s, s_new = _ret_sample(zr_s.reshape(DB, L, -1), zp_s.reshape(DB, L, -1), state_ret, l)
        decode_args = (page_table, zp_s[:, q_lo:q_lo + fox_w].reshape(DB, L * H, hd_fox),
                       fkb_s.reshape(DB, L * H, hd_fox), fvb_s.reshape(DB, L * H, hd_fox),
                       cn.reshape(DB, L, H), cn.reshape(DB, 1, L * H), cache_k, cache_v, w_sfx, l)
        half = DB // 2

        (xp, xpb), fox_s0 = _ffn_sublayer_with_decode(
            xp, wg1, wu1, wd1, *ln1, *decode_args, alpha=alpha, tm=tm_p, tf=tf, emit_bf16=True,
            pages=pages_per_step, batch0=0, n_batches=half)

        zr = _proj_rope(xpb, win, cos_p, sin_p, tm=tm_proj, tn=cw, heads=RET_HEADS, k_scale=k_scale)
        zp = _proj(xpb, win, tm=tm_proj, tn=cw, col0=2, n_blocks=3)
        fk, fkb = _proj_rows(xpb, win, tm=tm_proj, heads=H, hd=hd_fox, col0=5)
        fv, fvb = _proj_rows(xpb, win, tm=tm_proj, heads=H, hd=hd_fox, col0=6)
        logf, c, ct = _fgate_prompt(xpb.reshape(B, S, D), w_f, w_f.T, bf_row, bf_col)
        ret_o, s_fin = _ret_prompt(zr, zp, B=B, S=S, seq_blocks=2)
        fox_o = _fox_prompt(zp.reshape(B, S, -1), fkb.reshape(B, S, fox_w), fvb.reshape(B, S, fox_w), c, ct,
                            tq=tq, q_col=q_lo // fox_w)
        xp = _mix_out(ret_o, fox_o.reshape(Tp, fox_w), wo, xp, *ln2, alpha=alpha, tm=tm_p)
        outs["rs_p"].append(s_fin)
        outs["k_p"].append(fk.reshape(B, S, H, hd_fox))
        outs["v_p"].append(fv.reshape(B, S, H, hd_fox))
        outs["lf_p"].append(logf)

        (xp,), fox_s1 = _ffn_sublayer_with_decode(
            xp, wg2, wu2, wd2, *ln3, *decode_args, alpha=alpha, tm=tm_p, tf=tf, emit_bf16=False,
            pages=pages_per_step, batch0=half, n_batches=DB - half)

        fox_s = jnp.concatenate([fox_s0, fox_s1], axis=0)
        xs = _mix_out(ret_s.reshape(Ts, ret_w), fox_s.reshape(Ts, fox_w), wo, xs, *ln2, alpha=alpha, tm=Ts)
        (xs,) = _ffn_sublayer(xs, wg2, wu2, wd2, *ln3, alpha=alpha, tm=Ts, tf=tf, emit_bf16=False)
        outs["rs_s"].append(s_new)
        outs["k_s"].append(fk_s.reshape(DB, L, H, hd_fox))
        outs["v_s"].append(fv_s.reshape(DB, L, H, hd_fox))
        outs["lf_s"].append(logf_s.reshape(DB, L, H))

    stack = lambda k: outs[k][0][None] if depth == 1 else jnp.stack(outs[k])
    return (xp.reshape(B, S, D), xs.reshape(DB, L, D),
            stack("rs_p"), stack("k_p"), stack("v_p"), stack("lf_p"),
            stack("rs_s"), stack("k_s"), stack("v_s"), stack("lf_s"))
```

```python
import functools

import numpy as np
import jax
import jax.numpy as jnp
from jax import lax
from jax.experimental import pallas as pl
from jax.experimental.pallas import tpu as pltpu

F32 = jnp.float32
BF16 = jnp.bfloat16

RET_HEADS = 4
FOX_HEADS = 8
RET_CHUNK = 128
ROPE_BASE = 10000.0
LN_EPS = 1e-5
GN_EPS = 1e-6
LOG2E = float(np.log2(np.e))

LANES = 128
V7X_VMEM_BYTES = 64 * 1024 * 1024
MAX_SCOPED_VMEM_BYTES = 58 * 1024 * 1024
COMPILER_SCRATCH_BYTES = 12 * 1024 * 1024


def _nbytes(shape, dtype):
    return int(np.prod(shape)) * jnp.dtype(dtype).itemsize


def _params(semantics, block_bytes, scratch_bytes=0):
    need = 2 * block_bytes + scratch_bytes + COMPILER_SCRATCH_BYTES
    limit = int(min(need, MAX_SCOPED_VMEM_BYTES))
    return pltpu.CompilerParams(dimension_semantics=semantics, vmem_limit_bytes=limit)


def _layer_norm(y, g, b):
    mu = jnp.mean(y, axis=-1, keepdims=True)
    yc = y - mu
    var = jnp.mean(yc * yc, axis=-1, keepdims=True)
    return yc * lax.rsqrt(var + LN_EPS) * g + b


def _silu(x):
    return x * jax.nn.sigmoid(x)


def _log_sigmoid(x):
    return jnp.minimum(x, 0.0) - jnp.log1p(jnp.exp(-jnp.abs(x)))


def _dot(a, b):
    return jnp.dot(a, b, preferred_element_type=F32)


def _dot_nt(a, b):
    return lax.dot_general(a, b, (((1,), (1,)), ((), ())), preferred_element_type=F32)


def _dot_tn(a, b):
    return lax.dot_general(a, b, (((0,), (0,)), ((), ())), preferred_element_type=F32)


def _cumsum(x, axis, period=None):
    n = x.shape[axis]
    idx = lax.broadcasted_iota(jnp.int32, x.shape, axis)
    if period is not None:
        idx = idx % period
        n = period
    shift = 1
    while shift < n:
        x = x + jnp.where(idx >= shift, pltpu.roll(x, shift, axis), 0.0)
        shift *= 2
    return x


def _ffn_step(x_ref, wg_ref, wu_ref, wd_ref, g_ref, b_ref, out_refs, xb_ref, *, alpha, rider=None):
    f = pl.program_id(1)
    acc_ref = out_refs[0]

    @pl.when(f == 0)
    def _():
        xb_ref[...] = x_ref[...].astype(BF16)
        acc_ref[...] = jnp.zeros_like(acc_ref)

    if rider is not None:
        rider[0]()
    xb = xb_ref[...]
    hidden = _silu(_dot(xb, wg_ref[...])) * _dot(xb, wu_ref[...])
    acc_ref[...] += _dot(hidden.astype(BF16), wd_ref[...])
    if rider is not None:
        rider[1]()

    @pl.when(f == pl.num_programs(1) - 1)
    def _():
        y = _layer_norm(alpha * x_ref[...] + 0.5 * acc_ref[...], g_ref[...], b_ref[...])
        for o_ref in out_refs:
            o_ref[...] = y.astype(o_ref.dtype)

    if rider is not None:
        rider[2]()


def _ffn_kernel(x_ref, wg_ref, wu_ref, wd_ref, g_ref, b_ref, *rest, alpha):
    *out_refs, xb_ref = rest
    _ffn_step(x_ref, wg_ref, wu_ref, wd_ref, g_ref, b_ref, out_refs, xb_ref, alpha=alpha)


def _ffn_specs(T, D, F, tm, tf, emit_bf16):
    assert T % tm == 0 and F % tf == 0
    token_tile = lambda i, f, *_: (i, 0)
    in_specs = [pl.BlockSpec((tm, D), token_tile),
                pl.BlockSpec((D, tf), lambda i, f, *_: (0, f)),
                pl.BlockSpec((D, tf), lambda i, f, *_: (0, f)),
                pl.BlockSpec((tf, D), lambda i, f, *_: (f, 0)),
                pl.BlockSpec((1, D), lambda i, f, *_: (0, 0)),
                pl.BlockSpec((1, D), lambda i, f, *_: (0, 0))]
    out_dtypes = (F32, BF16) if emit_bf16 else (F32,)
    out_specs = [pl.BlockSpec((tm, D), token_tile) for _ in out_dtypes]
    out_shape = [jax.ShapeDtypeStruct((T, D), dt) for dt in out_dtypes]
    scratch = [pltpu.VMEM((tm, D), BF16)]
    block_bytes = (_nbytes((tm, D), F32) + sum(_nbytes((tm, D), dt) for dt in out_dtypes)
                   + 3 * _nbytes((D, tf), BF16))
    scratch_bytes = _nbytes((tm, D), BF16)
    return in_specs, out_specs, out_shape, scratch, block_bytes, scratch_bytes


def _ffn_sublayer(x, wg, wu, wd, g, b, *, alpha, tm, tf, emit_bf16):
    T, D = x.shape
    F = wg.shape[1]
    in_specs, out_specs, out_shape, scratch, block_bytes, scratch_bytes = _ffn_specs(T, D, F, tm, tf, emit_bf16)
    return pl.pallas_call(
        functools.partial(_ffn_kernel, alpha=alpha),
        grid=(T // tm, F // tf),
        in_specs=in_specs,
        out_specs=out_specs,
        out_shape=out_shape,
        scratch_shapes=scratch,
        compiler_params=_params(("parallel", "arbitrary"), block_bytes, scratch_bytes),
        name="ffn_sublayer",
    )(x, wg, wu, wd, g, b)


def _proj_rope_kernel(h_ref, w_ref, cos_ref, sin_ref, o_ref, *, heads, k_scale):
    scale = jnp.where(pl.program_id(0) == 0, 1.0, k_scale).astype(F32)
    z = _dot(h_ref[...], w_ref[...])
    cos = cos_ref[...]
    sin = sin_ref[...]
    hd = z.shape[1] // heads
    half = hd // 2
    for h in range(heads):
        x1 = z[:, h * hd:h * hd + half]
        x2 = z[:, h * hd + half:(h + 1) * hd]
        o_ref[:, h * hd:h * hd + half] = ((x1 * cos - x2 * sin) * scale).astype(o_ref.dtype)
        o_ref[:, h * hd + half:(h + 1) * hd] = ((x1 * sin + x2 * cos) * scale).astype(o_ref.dtype)


def _proj_rope(hb, w, cos, sin, *, tm, tn, heads, k_scale):
    T, D = hb.shape
    N = 2 * tn
    half = cos.shape[1]
    pos_blocks = cos.shape[0] // tm
    block_bytes = (_nbytes((tm, D), BF16) + _nbytes((D, tn), BF16) + 2 * _nbytes((tm, half), F32)
                   + _nbytes((tm, tn), BF16))
    return pl.pallas_call(
        functools.partial(_proj_rope_kernel, heads=heads, k_scale=k_scale),
        grid=(2, T // tm),
        in_specs=[pl.BlockSpec((tm, D), lambda j, i: (i, 0)),
                  pl.BlockSpec((D, tn), lambda j, i: (0, j)),
                  pl.BlockSpec((tm, half), lambda j, i: (i % pos_blocks, 0)),
                  pl.BlockSpec((tm, half), lambda j, i: (i % pos_blocks, 0))],
        out_specs=pl.BlockSpec((tm, tn), lambda j, i: (i, j)),
        out_shape=jax.ShapeDtypeStruct((T, N), BF16),
        compiler_params=_params(("parallel", "parallel"), block_bytes),
        name="proj_rope",
    )(hb, w, cos, sin)


def _proj_kernel(h_ref, w_ref, o_ref):
    o_ref[...] = _dot(h_ref[...], w_ref[...]).astype(o_ref.dtype)


def _proj(hb, w, *, tm, tn, col0, n_blocks):
    T, D = hb.shape
    N = n_blocks * tn
    block_bytes = _nbytes((tm, D), BF16) + _nbytes((D, tn), BF16) + _nbytes((tm, tn), BF16)
    return pl.pallas_call(
        _proj_kernel,
        grid=(n_blocks, T // tm),
        in_specs=[pl.BlockSpec((tm, D), lambda j, i: (i, 0)),
                  pl.BlockSpec((D, tn), lambda j, i: (0, col0 + j))],
        out_specs=pl.BlockSpec((tm, tn), lambda j, i: (i, j)),
        out_shape=jax.ShapeDtypeStruct((T, N), BF16),
        compiler_params=_params(("parallel", "parallel"), block_bytes),
        name="proj",
    )(hb, w)


def _proj_rows_kernel(h_ref, w_ref, rows_ref, ob_ref, *, heads):
    z = _dot(h_ref[...], w_ref[...])
    ob_ref[...] = z.astype(ob_ref.dtype)
    hd = z.shape[1] // heads
    per_head = jnp.stack([z[:, h * hd:(h + 1) * hd] for h in range(heads)], axis=0)
    rows_ref[...] = jnp.swapaxes(per_head, 0, 1)


def _proj_rows(hb, w, *, tm, heads, hd, col0):
    T, D = hb.shape
    N = heads * hd
    block_bytes = (_nbytes((tm, D), BF16) + _nbytes((D, N), BF16) + _nbytes((tm, N), F32)
                   + _nbytes((tm, N), BF16))
    return pl.pallas_call(
        functools.partial(_proj_rows_kernel, heads=heads),
        grid=(T // tm,),
        in_specs=[pl.BlockSpec((tm, D), lambda i: (i, 0)),
                  pl.BlockSpec((D, N), lambda i: (0, col0))],
        out_specs=[pl.BlockSpec((tm, heads, hd), lambda i: (i, 0, 0)),
                   pl.BlockSpec((tm, N), lambda i: (i, 0))],
        out_shape=[jax.ShapeDtypeStruct((T, heads, hd), F32),
                   jax.ShapeDtypeStruct((T, N), BF16)],
        compiler_params=_params(("parallel",), block_bytes),
        name="proj_rows",
    )(hb, w)


def _fgate_prompt_kernel(h_ref, wf_ref, wft_ref, brow_ref, bcol_ref, logf_ref, c_ref, ct_ref):
    h = h_ref[0]
    logf = _log_sigmoid(_dot(h, wf_ref[...]) + brow_ref[...])
    logf_ref[0] = logf
    c_ref[0] = _cumsum(logf, 0)
    logf_t = _log_sigmoid(_dot_nt(wft_ref[...], h) + bcol_ref[...])
    ct_ref[0] = _cumsum(logf_t, 1)


def _fgate_prompt(hb3, wf, wft, brow, bcol):
    B, S, D = hb3.shape
    H = wf.shape[1]
    block_bytes = _nbytes((S, D), BF16) + 3 * _nbytes((S, LANES), F32)
    return pl.pallas_call(
        _fgate_prompt_kernel,
        grid=(B,),
        in_specs=[pl.BlockSpec((1, S, D), lambda b: (b, 0, 0)),
                  pl.BlockSpec((D, H), lambda b: (0, 0)),
                  pl.BlockSpec((H, D), lambda b: (0, 0)),
                  pl.BlockSpec((1, H), lambda b: (0, 0)),
                  pl.BlockSpec((H, 1), lambda b: (0, 0))],
        out_specs=[pl.BlockSpec((1, S, H), lambda b: (b, 0, 0)),
                   pl.BlockSpec((1, S, H), lambda b: (b, 0, 0)),
                   pl.BlockSpec((1, H, S), lambda b: (b, 0, 0))],
        out_shape=[jax.ShapeDtypeStruct((B, S, H), F32),
                   jax.ShapeDtypeStruct((B, S, H), F32),
                   jax.ShapeDtypeStruct((B, H, S), F32)],
        compiler_params=_params(("parallel",), block_bytes),
        name="fgate_prompt",
    )(hb3, wf, wft, brow, bcol)


def _fgate_sample_kernel(h_ref, wf_ref, brow_ref, logf_ref, cn_ref, *, seq):
    logf = _log_sigmoid(_dot(h_ref[...], wf_ref[...]) + brow_ref[...])
    logf_ref[...] = logf
    cn_ref[...] = _cumsum(logf, 0, period=seq)


def _fgate_sample(hb, wf, brow, *, seq):
    T, _ = hb.shape
    H = wf.shape[1]
    return pl.pallas_call(
        functools.partial(_fgate_sample_kernel, seq=seq),
        out_shape=[jax.ShapeDtypeStruct((T, H), F32), jax.ShapeDtypeStruct((T, H), F32)],
        name="fgate_sample",
    )(hb, wf, brow)


def _ret_log_gamma(h):
    return jnp.float32(np.log1p(-np.exp2(-5.0 - h)))


def _ret_decays(log_g, C):
    row = lax.broadcasted_iota(jnp.int32, (C, C), 0)
    col = lax.broadcasted_iota(jnp.int32, (C, C), 1)
    rel = (row - col).astype(F32)
    inner = jnp.where(rel >= 0, jnp.exp(log_g * jnp.maximum(rel, 0.0)), 0.0)
    idx = lax.broadcasted_iota(jnp.int32, (C, 1), 0).astype(F32)
    q_decay = jnp.exp(log_g * (idx + 1.0))
    k_decay = jnp.exp(log_g * (C - 1.0 - idx))
    chunk_decay = jnp.exp(log_g * jnp.full((1, 1), float(C), F32))
    return inner, q_decay, k_decay, chunk_decay


def _ret_chunk(heads):
    first = []
    for q, k, v, g, state, (inner, q_decay, k_decay, chunk_decay) in heads:
        att = _dot_nt(q, k) * inner
        cross = _dot(q, state.astype(BF16)) * q_decay
        kd = (k.astype(F32) * k_decay).astype(BF16)
        first.append((att, cross, chunk_decay * state + _dot_tn(kd, v)))
    outs = []
    for (q, k, v, g, state, _), (att, cross, new_state) in zip(heads, first):
        o = _dot(att.astype(BF16), v) + cross
        rn = o * lax.rsqrt(jnp.mean(o * o, axis=-1, keepdims=True) + GN_EPS)
        outs.append(((rn * _silu(g.astype(F32))).astype(BF16), new_state))
    return outs


def _ret_prompt_kernel(q_ref, k_ref, v_ref, g_ref, o_ref, s_ref, *, chunk):
    rows, W = q_ref.shape
    hd = W // RET_HEADS
    decays = [_ret_decays(_ret_log_gamma(h), chunk) for h in range(RET_HEADS)]

    @pl.when(pl.program_id(1) == 0)
    def _():
        s_ref[...] = jnp.zeros_like(s_ref)

    def body(c, carry):
        sl = pl.ds(pl.multiple_of(c * chunk, chunk), chunk)
        cols = [slice(h * hd, (h + 1) * hd) for h in range(RET_HEADS)]
        outs = _ret_chunk([(q_ref[sl, c], k_ref[sl, c], v_ref[sl, c], g_ref[sl, c], s_ref[0, h], decays[h])
                           for h, c in enumerate(cols)])
        for h, (out, new_state) in enumerate(outs):
            o_ref[sl, cols[h]] = out
            s_ref[0, h] = new_state
        return carry

    lax.fori_loop(0, rows // chunk, body, 0)


def _ret_prompt(zr, zp, *, B, S, seq_blocks):
    W = zr.shape[1] // 2
    hd = W // RET_HEADS
    chunk = RET_CHUNK if S % RET_CHUNK == 0 else S
    rows = S // seq_blocks
    assert S % seq_blocks == 0 and rows % chunk == 0
    block_bytes = 5 * _nbytes((rows, W), BF16) + _nbytes((RET_HEADS, hd, hd), F32)
    tokens = lambda col: (lambda b, i: (b * seq_blocks + i, col))
    return pl.pallas_call(
        functools.partial(_ret_prompt_kernel, chunk=chunk),
        grid=(B, seq_blocks),
        in_specs=[pl.BlockSpec((rows, W), tokens(0)),
                  pl.BlockSpec((rows, W), tokens(1)),
                  pl.BlockSpec((rows, W), tokens(0)),
                  pl.BlockSpec((rows, W), tokens(1))],
        out_specs=[pl.BlockSpec((rows, W), tokens(0)),
                   pl.BlockSpec((1, RET_HEADS, hd, hd), lambda b, i: (b, 0, 0, 0))],
        out_shape=[jax.ShapeDtypeStruct((B * S, W), BF16),
                   jax.ShapeDtypeStruct((B, RET_HEADS, hd, hd), F32)],
        compiler_params=_params(("parallel", "arbitrary"), block_bytes),
        name="retention_prompt",
    )(zr, zr, zp, zp)


def _ret_sample_kernel(zr_ref, zp_ref, s0_ref, o_ref, s_ref, *, width):
    L = zr_ref.shape[1]
    hd = width // RET_HEADS
    heads = []
    for h in range(RET_HEADS):
        lo, hi = h * hd, (h + 1) * hd
        heads.append((zr_ref[0, :, lo:hi], zr_ref[0, :, width + lo:width + hi],
                      zp_ref[0, :, lo:hi], zp_ref[0, :, width + lo:width + hi],
                      s0_ref[0, 0, h], _ret_decays(_ret_log_gamma(h), L)))
    for h, (out, new_state) in enumerate(_ret_chunk(heads)):
        o_ref[0, :, h * hd:(h + 1) * hd] = out
        s_ref[0, h] = new_state


def _ret_sample(zr3, zp3, state, layer):
    DB, L, W2 = zr3.shape
    W = W2 // 2
    hd = W // RET_HEADS
    NP = zp3.shape[2]
    block_bytes = _nbytes((16, W2 + NP + W), BF16) + 2 * _nbytes((RET_HEADS, hd, hd), F32)
    return pl.pallas_call(
        functools.partial(_ret_sample_kernel, width=W),
        grid=(DB,),
        in_specs=[pl.BlockSpec((1, L, W2), lambda b: (b, 0, 0)),
                  pl.BlockSpec((1, L, NP), lambda b: (b, 0, 0)),
                  pl.BlockSpec((1, 1, RET_HEADS, hd, hd), lambda b: (layer, b, 0, 0, 0))],
        out_specs=[pl.BlockSpec((1, L, W), lambda b: (b, 0, 0)),
                   pl.BlockSpec((1, RET_HEADS, hd, hd), lambda b: (b, 0, 0, 0))],
        out_shape=[jax.ShapeDtypeStruct((DB, L, W), BF16),
                   jax.ShapeDtypeStruct((DB, RET_HEADS, hd, hd), F32)],
        compiler_params=_params(("parallel",), block_bytes),
        name="retention_sample",
    )(zr3, zp3, state)


def _fox_prompt_kernel(q_ref, k_ref, v_ref, c_ref, ct_ref, o_ref, qt_ref, vt_ref, acc_ref, m_ref, l_ref,
                       u_ref, p_ref, *, heads, scale):
    tq = q_ref.shape[1]
    S = k_ref.shape[1]
    hd = q_ref.shape[2] // heads
    qi = pl.program_id(1)
    a_scale = scale * LOG2E

    def transposed(x):
        return x.astype(F32).T.astype(BF16)

    @pl.when(qi == 0)
    def _():
        def body(j, carry):
            ks = pl.ds(pl.multiple_of(j * tq, tq), tq)
            for h in range(heads):
                vt_ref[h * hd:(h + 1) * hd, ks] = transposed(v_ref[0, ks, h * hd:(h + 1) * hd])
            return carry
        lax.fori_loop(0, S // tq, body, 0)

    for h in range(heads):
        qt_ref[h * hd:(h + 1) * hd, :] = transposed(q_ref[0, :, h * hd:(h + 1) * hd])
    qs = pl.ds(pl.multiple_of(qi * tq, tq), tq)
    c_q = ct_ref[0, :, qs] * LOG2E
    key = lax.broadcasted_iota(jnp.int32, (tq, tq), 0)
    qry = lax.broadcasted_iota(jnp.int32, (tq, tq), 1)
    causal = key <= qry

    def block(j, first):
        ks = pl.ds(pl.multiple_of(j * tq, tq), tq)
        c_k = c_ref[0, ks, :] * LOG2E
        maxes = []
        for h in range(heads):
            rows = slice(h * hd, (h + 1) * hd)
            u = _dot(k_ref[0, ks, rows], qt_ref[rows, :]) * a_scale - c_k[:, h:h + 1]
            if first:
                u = jnp.where(causal, u, -jnp.inf)
            u_ref[h] = u
            maxes.append(jnp.max(u, axis=0, keepdims=True))
        rescale = []
        for h in range(heads):
            cq = c_q[h:h + 1, :]
            mx = maxes[h] + cq
            if first:
                m_new = mx
            else:
                m_old = m_ref[h:h + 1, :]
                m_new = jnp.maximum(m_old, mx)
            p = jnp.exp2(u_ref[h] + (cq - m_new))
            ps = jnp.sum(p, axis=0, keepdims=True)
            p_ref[h] = p.astype(BF16)
            if first:
                l_ref[h:h + 1, :] = ps
                rescale.append(None)
            else:
                a = jnp.exp2(m_old - m_new)
                l_ref[h:h + 1, :] = a * l_ref[h:h + 1, :] + ps
                rescale.append(a)
            m_ref[h:h + 1, :] = m_new
        for h in range(heads):
            rows = slice(h * hd, (h + 1) * hd)
            pv = _dot(vt_ref[rows, ks], p_ref[h])
            if first:
                acc_ref[rows, :] = pv
            else:
                acc_ref[rows, :] = rescale[h] * acc_ref[rows, :] + pv

    block(qi, True)

    def off_diagonal(j, carry):
        block(j, False)
        return carry

    lax.fori_loop(0, qi, off_diagonal, 0)
    for h in range(heads):
        rows = slice(h * hd, (h + 1) * hd)
        o_ref[0, :, rows] = (acc_ref[rows, :] / l_ref[h:h + 1, :]).T.astype(o_ref.dtype)


def _fox_prompt(fq, fkb, fvb, c, ct, *, tq, q_col):
    B, S, H = c.shape
    W = fkb.shape[2]
    scale = float((W // H) ** -0.5)
    block_bytes = (2 * _nbytes((tq, W), BF16) + 2 * _nbytes((S, W), BF16)
                   + _nbytes((S, LANES), F32) + _nbytes((H, S), F32))
    scratch_bytes = (_nbytes((W, tq), BF16) + _nbytes((W, S), BF16) + _nbytes((W, tq), F32)
                     + _nbytes((H, tq, tq), F32) + _nbytes((H, tq, tq), BF16))
    return pl.pallas_call(
        functools.partial(_fox_prompt_kernel, heads=H, scale=scale),
        grid=(B, S // tq),
        in_specs=[pl.BlockSpec((1, tq, W), lambda b, i: (b, i, q_col)),
                  pl.BlockSpec((1, S, W), lambda b, i: (b, 0, 0)),
                  pl.BlockSpec((1, S, W), lambda b, i: (b, 0, 0)),
                  pl.BlockSpec((1, S, H), lambda b, i: (b, 0, 0)),
                  pl.BlockSpec((1, H, S), lambda b, i: (b, 0, 0))],
        out_specs=pl.BlockSpec((1, tq, W), lambda b, i: (b, i, 0)),
        out_shape=jax.ShapeDtypeStruct((B, S, W), BF16),
        scratch_shapes=[pltpu.VMEM((W, tq), BF16),
                        pltpu.VMEM((W, S), BF16),
                        pltpu.VMEM((W, tq), F32),
                        pltpu.VMEM((H, tq), F32),
                        pltpu.VMEM((H, tq), F32),
                        pltpu.VMEM((H, tq, tq), F32),
                        pltpu.VMEM((H, tq, tq), BF16)],
        compiler_params=_params(("parallel", "arbitrary"), block_bytes, scratch_bytes),
        name="fox_prompt",
    )(fq, fkb, fvb, c, ct)


def _suffix_kernel(x_ref, m_ref, o_ref):
    x = x_ref[...]
    hi = x.astype(BF16)
    r1 = x - hi.astype(F32)
    mid = r1.astype(BF16)
    lo = (r1 - mid.astype(F32)).astype(BF16)
    m = m_ref[...]
    sums = _dot(hi, m) + _dot(mid, m) + _dot(lo, m)
    for c in range(o_ref.shape[1]):
        o_ref[:, c, :] = sums[:, c * LANES:(c + 1) * LANES]


def _suffix_selector(rows, heads):
    src = np.arange(rows * heads)
    dst = np.arange(rows * heads)
    same_head = (src % heads)[:, None] == (dst % heads)[None, :]
    later = (src // heads)[:, None] > (dst // heads)[None, :]
    total = (src % heads)[:, None] == (np.arange(LANES) % heads)[None, :]
    return jnp.asarray(np.concatenate([same_head & later, total], axis=1), dtype=BF16)


def _page_suffix_sums(logf_flat, sel, *, tp):
    P, K = logf_flat.shape
    N = sel.shape[1]
    block_bytes = _nbytes((tp, K), F32) + _nbytes((tp, 2 * N), F32) + _nbytes((K, N), BF16)
    return pl.pallas_call(
        _suffix_kernel,
        grid=(P // tp,),
        in_specs=[pl.BlockSpec((tp, K), lambda i: (i, 0)),
                  pl.BlockSpec((K, N), lambda i: (0, 0))],
        out_specs=pl.BlockSpec((tp, N // LANES, LANES), lambda i: (i, 0, 0)),
        out_shape=jax.ShapeDtypeStruct((P, N // LANES, LANES), F32),
        compiler_params=_params(("parallel",), block_bytes),
        name="page_suffix_sums",
    )(logf_flat, sel)


def _decode_rider(g, last_g, active, q_ref, kn_ref, vn_ref, cn_ref, cnrow_ref, k_refs, v_refs, w_refs, o_ref,
                  cq_ref, m_ref, l_ref, acc_ref, carry_ref, *, heads, seq, scale):
    pages = len(k_refs)
    rows = seq * heads
    hd = q_ref.shape[2]
    n_keys = k_refs[0].shape[0] * heads
    chunks = n_keys // LANES
    a_scale = scale * LOG2E

    def before():
        @pl.when(jnp.logical_and(g == 0, active))
        def _():
            q = q_ref[0]
            cn = cn_ref[0]
            sub = lax.broadcasted_iota(jnp.int32, (heads, heads), 0)
            ln = lax.broadcasted_iota(jnp.int32, (heads, heads), 1)
            c_q = jnp.concatenate(
                [jnp.sum(jnp.where(sub == ln, jnp.broadcast_to(cn[t:t + 1, :], (heads, heads)), 0.0),
                         axis=1, keepdims=True) for t in range(seq)], axis=0)
            c_q = c_q * LOG2E
            cq_ref[...] = jnp.broadcast_to(c_q, cq_ref.shape)
            s = _dot_nt(q, kn_ref[0]) * a_scale - cnrow_ref[0] * LOG2E
            r_i = lax.broadcasted_iota(jnp.int32, s.shape, 0)
            k_i = lax.broadcasted_iota(jnp.int32, s.shape, 1)
            keep = (r_i % heads == k_i % heads) & (k_i // heads <= r_i // heads)
            s = jnp.where(keep, s, -jnp.inf)
            m0 = jnp.max(s, axis=-1, keepdims=True) + c_q
            p = jnp.exp2(s + (c_q - m0))
            m_ref[...] = jnp.broadcast_to(m0, m_ref.shape)
            l_ref[...] = jnp.broadcast_to(jnp.sum(p, axis=-1, keepdims=True), l_ref.shape)
            acc_ref[...] = _dot(p.astype(BF16), vn_ref[0])
            carry_ref[...] = jnp.zeros_like(carry_ref)

    def main():
        pl.when(active)(update)

    def update():
        q = q_ref[0]
        same_head = (lax.broadcasted_iota(jnp.int32, (rows, n_keys), 0) % heads
                     == lax.broadcasted_iota(jnp.int32, (rows, n_keys), 1) % heads)
        c_q = cq_ref[...][:, 0:1]
        m = m_ref[...][:, 0:1]
        carry = carry_ref[...]
        s_pages = [None] * pages
        for i in reversed(range(pages)):
            w = w_refs[i][...]
            bias = jnp.concatenate([w[c:c + 1, :] + carry for c in range(chunks)], axis=1) * LOG2E
            carry = carry + w[chunks:chunks + 1, :]
            k2 = k_refs[i][...].reshape(n_keys, hd).astype(BF16)
            s_pages[i] = jnp.where(same_head, _dot_nt(q, k2) * a_scale + bias, -jnp.inf)
        s_max = s_pages[0]
        for i in range(1, pages):
            s_max = jnp.maximum(s_max, s_pages[i])
        m_new = jnp.maximum(m, jnp.max(s_max, axis=-1, keepdims=True) + c_q)
        a = jnp.exp2(m - m_new)
        shift = c_q - m_new
        p_sum = jnp.zeros((rows, n_keys), F32)
        acc = a * acc_ref[...]
        for i in range(pages):
            p = jnp.exp2(s_pages[i] + shift)
            p_sum = p_sum + p
            acc = acc + _dot(p.astype(BF16), v_refs[i][...].reshape(n_keys, hd).astype(BF16))
        m_ref[...] = jnp.broadcast_to(m_new, m_ref.shape)
        l_ref[...] = a * l_ref[...] + jnp.sum(p_sum, axis=-1, keepdims=True)
        acc_ref[...] = acc
        carry_ref[...] = carry

    def after():
        @pl.when(jnp.logical_and(g == last_g, active))
        def _():
            o_ref[0] = (acc_ref[...] / l_ref[...][:, 0:1]).astype(o_ref.dtype)

    return before, main, after


def _ffn_decode_kernel(pt_ref, x_ref, wg_ref, wu_ref, wd_ref, g_ref, b_ref,
                       q_ref, kn_ref, vn_ref, cn_ref, cnrow_ref, *rest,
                       alpha, n_ffn_out, pages, groups, decode_steps, layer, batch0, heads, seq, scale):
    k_hbm, v_hbm, w_hbm = rest[:3]
    rest = rest[3:]
    out_refs = rest[:n_ffn_out]
    od_ref, xb_ref, cq_ref, m_ref, l_ref, dacc_ref, carry_ref, kbuf, vbuf, wbuf, sems = rest[n_ffn_out:]
    step = pl.program_id(0) * pl.num_programs(1) + pl.program_id(1)

    def page_copies(d, slot):
        bd = batch0 + d // groups
        first = (groups - 1 - lax.rem(d, groups)) * pages
        copies = []
        for p in range(pages):
            page = pt_ref[bd, first + p]
            copies.append(pltpu.make_async_copy(k_hbm.at[layer, page], kbuf.at[slot, p], sems.at[0, slot]))
            copies.append(pltpu.make_async_copy(v_hbm.at[layer, page], vbuf.at[slot, p], sems.at[1, slot]))
            copies.append(pltpu.make_async_copy(w_hbm.at[layer, page], wbuf.at[slot, p], sems.at[2, slot]))
        return copies

    @pl.when(step == 0)
    def _():
        for c in page_copies(0, 0):
            c.start()

    @pl.when(step + 1 < decode_steps)
    def _():
        for c in page_copies(step + 1, lax.rem(step + 1, 2)):
            c.start()

    active = step < decode_steps
    slot = lax.rem(step, 2)
    before, update, after = _decode_rider(
        lax.rem(step, groups), groups - 1, active, q_ref, kn_ref, vn_ref, cn_ref, cnrow_ref,
        [kbuf.at[slot, p] for p in range(pages)], [vbuf.at[slot, p] for p in range(pages)],
        [wbuf.at[slot, p] for p in range(pages)], od_ref, cq_ref, m_ref, l_ref, dacc_ref, carry_ref,
        heads=heads, seq=seq, scale=scale)

    def main():
        @pl.when(active)
        def _():
            for c in page_copies(step, slot):
                c.wait()
        update()

    _ffn_step(x_ref, wg_ref, wu_ref, wd_ref, g_ref, b_ref, out_refs, xb_ref, alpha=alpha,
              rider=(before, main, after))


def _ffn_sublayer_with_decode(x, wg, wu, wd, g, b, page_table, q2, kn2, vn2, cn3, cn_row, cache_k, cache_v,
                              w_sfx, layer, *, alpha, tm, tf, emit_bf16, pages, batch0, n_batches):
    T, D = x.shape
    F = wg.shape[1]
    _, rows, hd = q2.shape
    _, L, H = cn3.shape
    n_pages = page_table.shape[1]
    page = cache_k.shape[2]
    wrows = w_sfx.shape[2]
    assert n_pages % pages == 0 and page * H == (wrows - 1) * LANES and LANES % H == 0
    groups = n_pages // pages
    n_f = F // tf
    decode_steps = n_batches * groups
    assert 2 <= decode_steps <= (T // tm) * n_f
    ffn_in, ffn_out, ffn_shape, ffn_scratch, block_bytes, scratch_bytes = _ffn_specs(T, D, F, tm, tf, emit_bf16)

    def seq_of_step(i, f):
        return jnp.minimum(i * n_f + f, decode_steps - 1) // groups

    per_seq = lambda i, f, pt: (batch0 + seq_of_step(i, f), 0, 0)
    in_hbm = pl.BlockSpec(memory_space=pl.ANY)
    grid_spec = pltpu.PrefetchScalarGridSpec(
        num_scalar_prefetch=1,
        grid=(T // tm, n_f),
        in_specs=ffn_in + [pl.BlockSpec((1, rows, hd), per_seq),
                           pl.BlockSpec((1, rows, hd), per_seq),
                           pl.BlockSpec((1, rows, hd), per_seq),
                           pl.BlockSpec((1, L, H), per_seq),
                           pl.BlockSpec((1, 1, rows), per_seq),
                           in_hbm, in_hbm, in_hbm],
        out_specs=ffn_out + [pl.BlockSpec((1, rows, hd), lambda i, f, pt: (seq_of_step(i, f), 0, 0))],
        scratch_shapes=ffn_scratch + [pltpu.VMEM((rows, LANES), F32),
                                      pltpu.VMEM((rows, LANES), F32),
                                      pltpu.VMEM((rows, LANES), F32),
                                      pltpu.VMEM((rows, hd), F32),
                                      pltpu.VMEM((1, LANES), F32),
                                      pltpu.VMEM((2, pages, page, H, hd), F32),
                                      pltpu.VMEM((2, pages, page, H, hd), F32),
                                      pltpu.VMEM((2, pages, wrows, LANES), F32),
                                      pltpu.SemaphoreType.DMA((3, 2))])
    scratch_bytes += 4 * pages * _nbytes((page, H, hd), F32) + 2 * pages * _nbytes((16, LANES), F32)
    outs = pl.pallas_call(
        functools.partial(_ffn_decode_kernel, alpha=alpha, n_ffn_out=len(ffn_out), pages=pages, groups=groups,
                          decode_steps=decode_steps, layer=layer, batch0=batch0, heads=H, seq=L,
                          scale=float(hd ** -0.5)),
        grid_spec=grid_spec,
        out_shape=ffn_shape + [jax.ShapeDtypeStruct((n_batches, rows, hd), BF16)],
        compiler_params=_params(("arbitrary", "arbitrary"), block_bytes, scratch_bytes),
        name="ffn_sublayer_with_decode",
    )(page_table, x, wg, wu, wd, g, b, q2, kn2, vn2, cn3, cn_row, cache_k, cache_v, w_sfx)
    return outs[:-1], outs[-1]


def _mix_out_kernel(ret_ref, fox_ref, wr_ref, wf_ref, x_ref, g_ref, b_ref, o_ref, *, alpha, row_chunks):
    rows = o_ref.shape[0] // row_chunks
    for r in range(row_chunks):
        sl = slice(r * rows, (r + 1) * rows)
        y = alpha * x_ref[sl, :] + _dot(ret_ref[sl, :], wr_ref[...]) + _dot(fox_ref[sl, :], wf_ref[...])
        o_ref[sl, :] = _layer_norm(y, g_ref[...], b_ref[...])


def _mix_out(ret, fox, w_out, x, g, b, *, alpha, tm):
    T, D = x.shape
    Wr = ret.shape[1]
    Wf = fox.shape[1]
    assert Wr == Wf and w_out.shape[0] == Wr + Wf
    block_bytes = (_nbytes((tm, Wr + Wf), BF16) + _nbytes((Wr + Wf, D), BF16) + 2 * _nbytes((tm, D), F32))
    return pl.pallas_call(
        functools.partial(_mix_out_kernel, alpha=alpha, row_chunks=max(1, tm // 128)),
        grid=(T // tm,),
        in_specs=[pl.BlockSpec((tm, Wr), lambda i: (i, 0)),
                  pl.BlockSpec((tm, Wf), lambda i: (i, 0)),
                  pl.BlockSpec((Wr, D), lambda i: (0, 0)),
                  pl.BlockSpec((Wf, D), lambda i: (1, 0)),
                  pl.BlockSpec((tm, D), lambda i: (i, 0)),
                  pl.BlockSpec((1, D), lambda i: (0, 0)),
                  pl.BlockSpec((1, D), lambda i: (0, 0))],
        out_specs=pl.BlockSpec((tm, D), lambda i: (i, 0)),
        out_shape=jax.ShapeDtypeStruct((T, D), F32),
        compiler_params=_params(("parallel",), block_bytes),
        name="mix_out",
    )(ret, fox, w_out, w_out, x, g, b)


def _rope_tables(pos, half):
    inv = ROPE_BASE ** (-jnp.arange(half, dtype=F32) / half)
    ang = pos.astype(F32)[:, None] * inv[None, :]
    return jnp.cos(ang), jnp.sin(ang)


def kernel(x_prompt, x_sample, cache_k, cache_v, cache_logf, state_ret, page_table, ln1_g, ln1_b, w_ffn1_gate, w_ffn1_up, w_ffn1_down, ln2_g, ln2_b, w_in, b_f, w_out, ln3_g, ln3_b, w_ffn2_gate, w_ffn2_up, w_ffn2_down):
    B, S, D = x_prompt.shape
    DB, L, _ = x_sample.shape
    depth, n_pool, page, H, hd_fox = cache_k.shape
    assert H == FOX_HEADS
    n_pages = page_table.shape[1]
    past_len = n_pages * page
    ret_w = RET_HEADS * state_ret.shape[-1]
    fox_w = H * hd_fox
    ret_hd = ret_w // RET_HEADS
    assert ret_w == fox_w
    cw = ret_w
    alpha = float((2 * depth) ** 0.25)
    Tp, Ts = B * S, DB * L

    tm_p = 512
    tf = 512
    tm_proj = 1024
    tq = 256
    pages_per_step = 8

    cos_p, sin_p = _rope_tables(jnp.arange(S), ret_hd // 2)
    cos_s, sin_s = _rope_tables(jnp.tile(past_len + jnp.arange(L), DB), ret_hd // 2)

    xp = x_prompt.reshape(Tp, D)
    xs = x_sample.reshape(Ts, D)
    w_sfx = _page_suffix_sums(cache_logf.reshape(depth * n_pool, page * H), _suffix_selector(page, H), tp=512)
    w_sfx = w_sfx.reshape(depth, n_pool, *w_sfx.shape[1:])

    outs = {k: [] for k in ("rs_p", "k_p", "v_p", "lf_p", "rs_s", "k_s", "v_s", "lf_s")}
    row = lambda v: v.reshape(1, -1)
    for l in range(depth):
        wg1, wu1, wd1 = (w.astype(BF16) for w in (w_ffn1_gate[l], w_ffn1_up[l], w_ffn1_down[l]))
        wg2, wu2, wd2 = (w.astype(BF16) for w in (w_ffn2_gate[l], w_ffn2_up[l], w_ffn2_down[l]))
        win = w_in[l].astype(BF16)
        w_f = win[:, 7 * cw:]
        bf_row = b_f[l].astype(F32).reshape(1, H)
        bf_col = b_f[l].astype(F32).reshape(H, 1)
        wo = w_out[l].astype(BF16)
        q_lo = 2 * cw

        ln1, ln2, ln3 = ((row(g_[l]), row(b_[l])) for g_, b_ in ((ln1_g, ln1_b), (ln2_g, ln2_b), (ln3_g, ln3_b)))
        k_scale = float(ret_hd ** -0.5)

        xs, xsb = _ffn_sublayer(xs, wg1, wu1, wd1, *ln1, alpha=alpha, tm=Ts, tf=tf, emit_bf16=True)
        zr_s = _proj_rope(xsb, win, cos_s, sin_s, tm=Ts, tn=cw, heads=RET_HEADS, k_scale=k_scale)
        zp_s = _proj(xsb, win, tm=Ts, tn=cw, col0=2, n_blocks=3)
        fk_s, fkb_s = _proj_rows(xsb, win, tm=Ts, heads=H, hd=hd_fox, col0=5)
        fv_s, fvb_s = _proj_rows(xsb, win, tm=Ts, heads=H, hd=hd_fox, col0=6)
        logf_s, cn = _fgate_sample(xsb, w_f, bf_row, seq=L)
        ret_s, s_new = _ret_sample(zr_s.reshape(DB, L, -1), zp_s.reshape(DB, L, -1), state_ret, l)
        decode_args = (page_table, zp_s[:, q_lo:q_lo + fox_w].reshape(DB, L * H, hd_fox),
                       fkb_s.reshape(DB, L * H, hd_fox), fvb_s.reshape(DB, L * H, hd_fox),
                       cn.reshape(DB, L, H), cn.reshape(DB, 1, L * H), cache_k, cache_v, w_sfx, l)
        half = DB // 2

        (xp, xpb), fox_s0 = _ffn_sublayer_with_decode(
            xp, wg1, wu1, wd1, *ln1, *decode_args, alpha=alpha, tm=tm_p, tf=tf, emit_bf16=True,
            pages=pages_per_step, batch0=0, n_batches=half)

        zr = _proj_rope(xpb, win, cos_p, sin_p, tm=tm_proj, tn=cw, heads=RET_HEADS, k_scale=k_scale)
        zp = _proj(xpb, win, tm=tm_proj, tn=cw, col0=2, n_blocks=3)
        fk, fkb = _proj_rows(xpb, win, tm=tm_proj, heads=H, hd=hd_fox, col0=5)
        fv, fvb = _proj_rows(xpb, win, tm=tm_proj, heads=H, hd=hd_fox, col0=6)
        logf, c, ct = _fgate_prompt(xpb.reshape(B, S, D), w_f, w_f.T, bf_row, bf_col)
        ret_o, s_fin = _ret_prompt(zr, zp, B=B, S=S, seq_blocks=2)
        fox_o = _fox_prompt(zp.reshape(B, S, -1), fkb.reshape(B, S, fox_w), fvb.reshape(B, S, fox_w), c, ct,
                            tq=tq, q_col=q_lo // fox_w)
        xp = _mix_out(ret_o, fox_o.reshape(Tp, fox_w), wo, xp, *ln2, alpha=alpha, tm=tm_p)
        outs["rs_p"].append(s_fin)
        outs["k_p"].append(fk.reshape(B, S, H, hd_fox))
        outs["v_p"].append(fv.reshape(B, S, H, hd_fox))
        outs["lf_p"].append(logf)

        (xp,), fox_s1 = _ffn_sublayer_with_decode(
            xp, wg2, wu2, wd2, *ln3, *decode_args, alpha=alpha, tm=tm_p, tf=tf, emit_bf16=False,
            pages=pages_per_step, batch0=half, n_batches=DB - half)

        fox_s = jnp.concatenate([fox_s0, fox_s1], axis=0)
        xs = _mix_out(ret_s.reshape(Ts, ret_w), fox_s.reshape(Ts, fox_w), wo, xs, *ln2, alpha=alpha, tm=Ts)
        (xs,) = _ffn_sublayer(xs, wg2, wu2, wd2, *ln3, alpha=alpha, tm=Ts, tf=tf, emit_bf16=False)
        outs["rs_s"].append(s_new)
        outs["k_s"].append(fk_s.reshape(DB, L, H, hd_fox))
        outs["v_s"].append(fv_s.reshape(DB, L, H, hd_fox))
        outs["lf_s"].append(logf_s.reshape(DB, L, H))

    stack = lambda k: outs[k][0][None] if depth == 1 else jnp.stack(outs[k])
    return (xp.reshape(B, S, D), xs.reshape(DB, L, D),
            stack("rs_p"), stack("k_p"), stack("v_p"), stack("lf_p"),
            stack("rs_s"), stack("k_s"), stack("v_s"), stack("lf_s"))
```
